```python
import math
import jax, jax.numpy as jnp
from jax import lax
import numpy as np

D_MODEL = 1024
BATCH = 2
SEQ = 16384
DEPTH = 2

PLE_DIM = 256
D_FF = 4 * D_MODEL
NORM_EPS = 1e-6
ROPE_THETA = 500000.0
ROPE_FRACTION = 4
Q_BLOCK = 128

SB_HEADS = 8
SB_HEAD_DIM = 64
DIFF_HEADS = 4
DIFF_HEAD_DIM = 64
DIFF_V_DIM = 2 * DIFF_HEAD_DIM
SB_WIDTH = SB_HEADS * SB_HEAD_DIM
DIFF_QK_WIDTH = DIFF_HEADS * 2 * DIFF_HEAD_DIM
DIFF_V_WIDTH = DIFF_HEADS * DIFF_V_DIM
EVEN_IN_WIDTH = 3 * SB_WIDTH + 2 * DIFF_QK_WIDTH + DIFF_V_WIDTH
EVEN_OUT_WIDTH = SB_WIDTH + DIFF_V_WIDTH

MOBA_HEADS = 16
MOBA_HEAD_DIM = 64
MOBA_WIDTH = MOBA_HEADS * MOBA_HEAD_DIM
MOBA_BLOCK = 256
MOBA_TOPK = 3
MOBA_Q_CHUNK = 64

N_EVEN = (DEPTH + 1) // 2
N_ODD = DEPTH // 2

kernel_name = "hybrid_stickbreak_diff_moba_trunk"


def rms_norm(x, g):
    xf = x.astype(jnp.float32)
    y = xf * lax.rsqrt(jnp.mean(xf * xf, axis=-1, keepdims=True) + NORM_EPS)
    return (y * g.astype(jnp.float32)).astype(x.dtype)


def rope_tables(positions, head_dim):
    rot = head_dim // ROPE_FRACTION
    inv_freq = ROPE_THETA ** (-(jnp.arange(0, rot, 2, dtype=jnp.float32) / rot))
    ang = positions.astype(jnp.float32)[..., None] * inv_freq
    return jnp.cos(ang), jnp.sin(ang)


def apply_partial_rope(x, cos, sin):
    half = cos.shape[-1]
    rot = 2 * half
    c = cos[:, :, None, :]
    s = sin[:, :, None, :]
    x1 = x[..., :half].astype(jnp.float32)
    x2 = x[..., half:rot].astype(jnp.float32)
    r = jnp.concatenate([x1 * c - x2 * s, x1 * s + x2 * c], axis=-1).astype(x.dtype)
    return jnp.concatenate([r, x[..., rot:]], axis=-1)


def to_heads(t, n_heads):
    b, s, _ = t.shape
    return t.reshape(b, s, n_heads, -1).transpose(0, 2, 1, 3)


def from_heads(t):
    b, h, s, d = t.shape
    return t.transpose(0, 2, 1, 3).reshape(b, s, h * d)


def stick_breaking_attention(q, k, v):
    B, H, S, d = q.shape
    nqb = S // Q_BLOCK
    scale = d ** -0.5
    kpos = jnp.arange(S)
    qb = q.reshape(B, H, nqb, Q_BLOCK, d).transpose(2, 0, 1, 3, 4)

    def one_block(args):
        qi, bi = args
        qpos = bi * Q_BLOCK + jnp.arange(Q_BLOCK)
        z = jnp.einsum('bhqd,bhkd->bhqk', qi, k).astype(jnp.float32) * scale
        past = kpos[None, :] < qpos[:, None]
        log_keep = jnp.where(past, jax.nn.log_sigmoid(-z), 0.0)
        tail = lax.cumsum(log_keep, axis=3, reverse=True) - log_keep
        w = jnp.where(past, jnp.exp(jax.nn.log_sigmoid(z) + tail), 0.0)
        return jnp.einsum('bhqk,bhkd->bhqd', w.astype(v.dtype), v)

    out = lax.map(one_block, (qb, jnp.arange(nqb)))
    return out.transpose(1, 2, 0, 3, 4).reshape(B, H, S, d)


def diff_attention(q1, q2, k1, k2, v, lam):
    B, H, S, d = q1.shape
    nqb = S // Q_BLOCK
    scale = d ** -0.5
    kpos = jnp.arange(S)
    q1b = q1.reshape(B, H, nqb, Q_BLOCK, d).transpose(2, 0, 1, 3, 4)
    q2b = q2.reshape(B, H, nqb, Q_BLOCK, d).transpose(2, 0, 1, 3, 4)
    lam_b = lam[None, :, None, None]

    def one_block(args):
        q1i, q2i, bi = args
        qpos = bi * Q_BLOCK + jnp.arange(Q_BLOCK)
        causal = kpos[None, :] <= qpos[:, None]

        def probs(qi, ki):
            s = jnp.einsum('bhqd,bhkd->bhqk', qi, ki).astype(jnp.float32) * scale
            return jax.nn.softmax(jnp.where(causal, s, -jnp.inf), axis=-1)

        a = probs(q1i, k1) - lam_b * probs(q2i, k2)
        return jnp.einsum('bhqk,bhkd->bhqd', a.astype(v.dtype), v)

    out = lax.map(one_block, (q1b, q2b, jnp.arange(nqb)))
    return out.transpose(1, 2, 0, 3, 4).reshape(B, H, S, v.shape[-1])


def moba_attention(q, k, v):
    B, H, S, d = q.shape
    nb = -(-S // MOBA_BLOCK)
    pad = nb * MOBA_BLOCK - S
    kp = jnp.pad(k, ((0, 0), (0, 0), (0, pad), (0, 0)))
    vp = jnp.pad(v, ((0, 0), (0, 0), (0, pad), (0, 0)))
    kblk = kp.reshape(B, H, nb, MOBA_BLOCK, d)
    vblk = vp.reshape(B, H, nb, MOBA_BLOCK, d)
    kmean = kblk.mean(axis=3)
    n_sel = min(MOBA_TOPK, nb)
    nqc = S // MOBA_Q_CHUNK
    scale = d ** -0.5
    qc = q.reshape(B, H, nqc, MOBA_Q_CHUNK, d).transpose(2, 0, 1, 3, 4)
    blk_ids = jnp.arange(nb)
    bidx = jnp.arange(B)[:, None, None, None]
    hidx = jnp.arange(H)[None, :, None, None]

    def one_chunk(args):
        qi, ci = args
        q0 = ci * MOBA_Q_CHUNK
        own = q0 // MOBA_BLOCK
        qpos = q0 + jnp.arange(MOBA_Q_CHUNK)
        gate = jnp.einsum('bhqd,bhnd->bhqn', qi, kmean).astype(jnp.float32)
        gate = jnp.where(blk_ids < own, gate, -jnp.inf)
        _, gidx = lax.top_k(gate, n_sel)
        sel_valid = gidx < own
        ksel = kblk[bidx, hidx, gidx]
        vsel = vblk[bidx, hidx, gidx]
        s_sel = jnp.einsum('bhqd,bhqnkd->bhqnk', qi, ksel).astype(jnp.float32) * scale
        s_sel = jnp.where(sel_valid[..., None], s_sel, -jnp.inf)
        s_sel = s_sel.reshape(B, H, MOBA_Q_CHUNK, n_sel * MOBA_BLOCK)
        k_own = lax.dynamic_slice_in_dim(kp, own * MOBA_BLOCK, MOBA_BLOCK, axis=2)
        v_own = lax.dynamic_slice_in_dim(vp, own * MOBA_BLOCK, MOBA_BLOCK, axis=2)
        own_pos = own * MOBA_BLOCK + jnp.arange(MOBA_BLOCK)
        s_own = jnp.einsum('bhqd,bhkd->bhqk', qi, k_own).astype(jnp.float32) * scale
        s_own = jnp.where(own_pos[None, :] <= qpos[:, None], s_own, -jnp.inf)
        probs = jax.nn.softmax(jnp.concatenate([s_sel, s_own], axis=-1), axis=-1)
        p_sel = probs[..., :n_sel * MOBA_BLOCK].reshape(B, H, MOBA_Q_CHUNK, n_sel, MOBA_BLOCK)
        p_own = probs[..., n_sel * MOBA_BLOCK:]
        return (jnp.einsum('bhqnk,bhqnkd->bhqd', p_sel.astype(v.dtype), vsel)
                + jnp.einsum('bhqk,bhkd->bhqd', p_own.astype(v.dtype), v_own))

    out = lax.map(one_chunk, (qc, jnp.arange(nqc)))
    return out.transpose(1, 2, 0, 3, 4).reshape(B, H, S, d)


def even_mixer(u, w_in, w_out, lq1, lk1, lq2, lk2, subln_g, cos, sin, layer_idx):
    B, S, _ = u.shape
    proj = u @ w_in
    cuts = list(np.cumsum([SB_WIDTH, SB_WIDTH, SB_WIDTH, DIFF_QK_WIDTH, DIFF_QK_WIDTH]))
    a_q, a_k, a_v, b_q, b_k, b_v = jnp.split(proj, [int(c) for c in cuts], axis=-1)
    o_a = stick_breaking_attention(to_heads(a_q, SB_HEADS), to_heads(a_k, SB_HEADS),
                                   to_heads(a_v, SB_HEADS))
    bq = b_q.reshape(B, S, DIFF_HEADS, 2, DIFF_HEAD_DIM)
    bk = b_k.reshape(B, S, DIFF_HEADS, 2, DIFF_HEAD_DIM)
    rq = lambda t: apply_partial_rope(t, cos, sin).transpose(0, 2, 1, 3)
    q1, q2 = rq(bq[..., 0, :]), rq(bq[..., 1, :])
    k1, k2 = rq(bk[..., 0, :]), rq(bk[..., 1, :])
    lam_init = 0.8 - 0.6 * math.exp(-0.3 * layer_idx)
    f32 = jnp.float32
    lam = (jnp.exp(jnp.sum(lq1.astype(f32) * lk1.astype(f32), axis=-1))
           - jnp.exp(jnp.sum(lq2.astype(f32) * lk2.astype(f32), axis=-1)) + lam_init)
    o_b = diff_attention(q1, q2, k1, k2, to_heads(b_v, DIFF_HEADS), lam)
    o_b = rms_norm(o_b, subln_g) * (1.0 - lam_init)
    merged = jnp.concatenate([from_heads(o_a), from_heads(o_b)], axis=-1)
    return merged @ w_out


def odd_mixer(u, w_in, w_out, cos, sin):
    B, S, _ = u.shape
    q, k, v = jnp.split(u @ w_in, 3, axis=-1)
    q = apply_partial_rope(q.reshape(B, S, MOBA_HEADS, MOBA_HEAD_DIM), cos, sin).transpose(0, 2, 1, 3)
    k = apply_partial_rope(k.reshape(B, S, MOBA_HEADS, MOBA_HEAD_DIM), cos, sin).transpose(0, 2, 1, 3)
    o = moba_attention(q, k, to_heads(v, MOBA_HEADS))
    return from_heads(o) @ w_out


def squared_relu_mlp(u, w1, w2):
    return jnp.square(jax.nn.relu(u @ w1)) @ w2


def setup_inputs(seed: int = 0) -> dict:
    key = jax.random.key(seed)
    ks = jax.random.split(key, 24)
    f32 = jnp.float32
    nrm = lambda k, shape, fan_in: jax.random.normal(k, shape, f32) * (fan_in ** -0.5)
    gain = lambda k, shape: 1.0 + 0.01 * jax.random.normal(k, shape, f32)
    x = jax.random.normal(ks[0], (BATCH, SEQ, D_MODEL), f32)
    p = jax.random.normal(ks[1], (DEPTH, BATCH, SEQ, PLE_DIM), f32)
    offsets = jax.random.randint(ks[2], (BATCH, 1), 0, 4096, dtype=jnp.int32)
    positions = (offsets + jnp.arange(SEQ, dtype=jnp.int32)[None, :]).astype(jnp.int32)
    return {
        "x": x,
        "p": p,
        "positions": positions,
        "attn_norm": gain(ks[3], (DEPTH, D_MODEL)),
        "ab_w_in": nrm(ks[4], (N_EVEN, D_MODEL, EVEN_IN_WIDTH), D_MODEL),
        "ab_w_out": nrm(ks[5], (N_EVEN, EVEN_OUT_WIDTH, D_MODEL), EVEN_OUT_WIDTH),
        "diff_lam_q1": 0.1 * jax.random.normal(ks[6], (N_EVEN, DIFF_HEADS, DIFF_HEAD_DIM), f32),
        "diff_lam_k1": 0.1 * jax.random.normal(ks[7], (N_EVEN, DIFF_HEADS, DIFF_HEAD_DIM), f32),
        "diff_lam_q2": 0.1 * jax.random.normal(ks[8], (N_EVEN, DIFF_HEADS, DIFF_HEAD_DIM), f32),
        "diff_lam_k2": 0.1 * jax.random.normal(ks[9], (N_EVEN, DIFF_HEADS, DIFF_HEAD_DIM), f32),
        "diff_subln": gain(ks[10], (N_EVEN, DIFF_V_DIM)),
        "moba_w_in": nrm(ks[11], (N_ODD, D_MODEL, 3 * MOBA_WIDTH), D_MODEL),
        "moba_w_out": nrm(ks[12], (N_ODD, MOBA_WIDTH, D_MODEL), MOBA_WIDTH),
        "mlp_norm": gain(ks[13], (DEPTH, D_MODEL)),
        "w_ff1": nrm(ks[14], (DEPTH, D_MODEL, D_FF), D_MODEL),
        "w_ff2": nrm(ks[15], (DEPTH, D_FF, D_MODEL), D_FF),
        "ple_norm": gain(ks[16], (DEPTH, D_MODEL)),
        "ple_gate": nrm(ks[17], (DEPTH, D_MODEL, D_MODEL), D_MODEL),
        "ple_proj": nrm(ks[18], (DEPTH, PLE_DIM, D_MODEL), PLE_DIM),
        "final_norm": gain(ks[19], (D_MODEL,)),
    }


def reference(x, p, positions, attn_norm, ab_w_in, ab_w_out, diff_lam_q1, diff_lam_k1,
              diff_lam_q2, diff_lam_k2, diff_subln, moba_w_in, moba_w_out, mlp_norm,
              w_ff1, w_ff2, ple_norm, ple_gate, ple_proj, final_norm):
    cos, sin = rope_tables(positions, DIFF_HEAD_DIM)
    h = x
    for i in range(DEPTH):
        u = rms_norm(h, attn_norm[i])
        if i % 2 == 0:
            j = i // 2
            mix = even_mixer(u, ab_w_in[j], ab_w_out[j], diff_lam_q1[j], diff_lam_k1[j],
                             diff_lam_q2[j], diff_lam_k2[j], diff_subln[j], cos, sin, i)
        else:
            j = i // 2
            mix = odd_mixer(u, moba_w_in[j], moba_w_out[j], cos, sin)
        h = h + mix
        h = h + squared_relu_mlp(rms_norm(h, mlp_norm[i]), w_ff1[i], w_ff2[i])
        gate = jax.nn.sigmoid(rms_norm(h, ple_norm[i]) @ ple_gate[i])
        h = h + gate * (p[i].astype(h.dtype) @ ple_proj[i])
    return rms_norm(h, final_norm)
```

```python
import functools
import math

import jax
import jax.numpy as jnp
from jax import lax
from jax.experimental import pallas as pl
from jax.experimental.pallas import tpu as pltpu

F32 = jnp.float32
BF16 = jnp.bfloat16

NORM_EPS = 1e-6
ROPE_THETA = 500000.0
HEAD_DIM = 64
ROPE_DIMS = HEAD_DIM // 4
ROPE_HALF = ROPE_DIMS // 2
LANES = 128
QK_SCALE = HEAD_DIM ** -0.5
SB_WIDTH = 512
DIFF_WIDTH = 512
DIFF_HEADS = 4
MOBA_WIDTH = 1024
MOBA_BLOCK = 256
MOBA_TOPK = 3
ATTN_BLOCK = 256
PROJ_CHUNK = 512
TOKEN_TILE = 512
MASK_BIAS = -1e30
VMEM_LIMIT = 48 * 1024 * 1024


def _params(n_axes):
    return pltpu.CompilerParams(dimension_semantics=("arbitrary",) * n_axes,
                                vmem_limit_bytes=VMEM_LIMIT)


def _resident(shape, index_map):
    return pl.BlockSpec(shape, index_map, pipeline_mode=pl.Buffered(1))


def _rope_table_kernel(pos_ref, invf_ref, cos_ref, sin_ref):
    ang = pos_ref[...].astype(F32) * invf_ref[...]
    lane = lax.broadcasted_iota(jnp.int32, ang.shape, 1) % HEAD_DIM
    c = jnp.cos(ang)
    s = jnp.sin(ang)
    cos_ref[...] = jnp.where(lane < ROPE_DIMS, c, 1.0)
    sin_ref[...] = jnp.where(lane < ROPE_HALF, -s, jnp.where(lane < ROPE_DIMS, s, 0.0))


def _rope_tables(positions):
    t = positions.size
    pos = positions.reshape(t, 1)
    inv_freq = ROPE_THETA ** (-(jnp.arange(0, ROPE_DIMS, 2, dtype=F32) / ROPE_DIMS))
    invf = jnp.tile(inv_freq, LANES // ROPE_HALF).reshape(1, LANES)
    tm = min(2048, t)
    return pl.pallas_call(
        _rope_table_kernel,
        grid=(t // tm,),
        in_specs=[pl.BlockSpec((tm, 1), lambda i: (i, 0)),
                  pl.BlockSpec((1, LANES), lambda i: (0, 0))],
        out_specs=[pl.BlockSpec((tm, LANES), lambda i: (i, 0))] * 2,
        out_shape=[jax.ShapeDtypeStruct((t, LANES), F32)] * 2,
        compiler_params=_params(1),
        name="rope_tables",
    )(pos, invf)


def _rope(a, cos, sin):
    lane = lax.broadcasted_iota(jnp.int32, a.shape, 1) % HEAD_DIM
    partner = jnp.where(lane < ROPE_HALF,
                        pltpu.roll(a, LANES - ROPE_HALF, 1),
                        pltpu.roll(a, ROPE_HALF, 1))
    return a * cos + partner * sin


def _rms(x, g):
    ms = jnp.mean(x * x, axis=-1, keepdims=True)
    return x * lax.rsqrt(ms + NORM_EPS) * g


def _norm_proj_kernel(x_ref, g_ref, w_ref, cos_ref, sin_ref, o_ref, *, rope_chunks, scale_chunks):
    u = _rms(x_ref[...], g_ref[...]).astype(BF16)
    n = w_ref.shape[1]
    for c in range(n // PROJ_CHUNK):
        cols = slice(c * PROJ_CHUNK, (c + 1) * PROJ_CHUNK)
        acc = jnp.dot(u, w_ref[:, cols], preferred_element_type=F32)
        if c in scale_chunks:
            acc = acc * QK_SCALE
        if c in rope_chunks:
            cos = cos_ref[...]
            sin = sin_ref[...]
            for gidx in range(PROJ_CHUNK // LANES):
                lanes = slice(gidx * LANES, (gidx + 1) * LANES)
                o_ref[:, c * PROJ_CHUNK + gidx * LANES:c * PROJ_CHUNK + (gidx + 1) * LANES] = (
                    _rope(acc[:, lanes], cos, sin).astype(BF16))
        else:
            o_ref[:, cols] = acc.astype(BF16)


def _norm_proj(h, g, w, cos_t, sin_t, rope_chunks, scale_chunks):
    t, d = h.shape
    n = w.shape[1]
    tm = min(TOKEN_TILE, t)
    kern = functools.partial(_norm_proj_kernel, rope_chunks=rope_chunks, scale_chunks=scale_chunks)
    return pl.pallas_call(
        kern,
        grid=(t // tm,),
        in_specs=[pl.BlockSpec((tm, d), lambda i: (i, 0)),
                  _resident((1, d), lambda i: (0, 0)),
                  _resident((d, n), lambda i: (0, 0)),
                  pl.BlockSpec((tm, LANES), lambda i: (i, 0)),
                  pl.BlockSpec((tm, LANES), lambda i: (i, 0))],
        out_specs=pl.BlockSpec((tm, n), lambda i: (i, 0)),
        out_shape=jax.ShapeDtypeStruct((t, n), BF16),
        compiler_params=_params(1),
        name="norm_proj",
    )(h, g.reshape(1, d), w, cos_t, sin_t)


def _head_lanes(shape, head):
    lane = lax.broadcasted_iota(jnp.int32, shape, 1)
    return (lane >= head * HEAD_DIM) & (lane < (head + 1) * HEAD_DIM)


def _qk(q, k):
    return lax.dot_general(q, k, (((1,), (1,)), ((), ())), preferred_element_type=F32)


def _softmax_step(s, v, m, l, acc):
    m_new = jnp.maximum(m, jnp.max(s, axis=-1, keepdims=True))
    alpha = jnp.exp(m - m_new)
    p = jnp.exp(s - m_new)
    l = alpha * l + jnp.sum(p, axis=-1, keepdims=True)
    acc = alpha * acc + jnp.dot(p.astype(BF16), v, preferred_element_type=F32)
    return m_new, l, acc


def _sb_kernel(q_ref, k_ref, v_ref, o_ref):
    qi = pl.program_id(2)
    blk = ATTN_BLOCK
    q = q_ref[0]
    row = lax.broadcasted_iota(jnp.int32, (blk, blk), 0)
    col = lax.broadcasted_iota(jnp.int32, (blk, blk), 1)
    past = col < row
    later = (row > col).astype(BF16)

    def block(qh, j, c, acc, diag):
        kj = k_ref[0, pl.ds(j * blk, blk), :]
        vj = v_ref[0, pl.ds(j * blk, blk), :]
        z = _qk(qh, kj)
        log_keep_all = -(jnp.maximum(z, 0.0) + jnp.log(1.0 + jnp.exp(-jnp.abs(z))))
        log_keep = jnp.where(past, log_keep_all, 0.0) if diag else log_keep_all
        hi = log_keep.astype(BF16)
        lo = (log_keep - hi.astype(F32)).astype(BF16)
        tail = (jnp.dot(hi, later, preferred_element_type=F32)
                + jnp.dot(lo, later, preferred_element_type=F32) + c)
        w = jnp.exp(log_keep_all + z + tail)
        if diag:
            w = jnp.where(past, w, 0.0)
        acc = acc + jnp.dot(w.astype(BF16), vj, preferred_element_type=F32)
        c = c + jnp.sum(log_keep, axis=-1, keepdims=True)
        return c, acc

    outs = []
    for head in range(2):
        qh = jnp.where(_head_lanes(q.shape, head), q, jnp.zeros_like(q))
        c0 = jnp.zeros((blk, 1), F32)
        acc0 = jnp.zeros((blk, LANES), F32)
        c, acc = block(qh, qi, c0, acc0, True)

        def body(t, carry, qh=qh):
            return block(qh, qi - 1 - t, carry[0], carry[1], False)

        c, acc = lax.fori_loop(0, qi, body, (c, acc))
        outs.append(acc)
    o_ref[0] = jnp.where(_head_lanes(outs[0].shape, 0), outs[0], outs[1]).astype(o_ref.dtype)


def _sb_attention(proj, n_batch, seq):
    groups = SB_WIDTH // LANES
    return pl.pallas_call(
        _sb_kernel,
        grid=(n_batch, groups, seq // ATTN_BLOCK),
        in_specs=[pl.BlockSpec((1, ATTN_BLOCK, LANES), lambda b, g, i: (b, i, g)),
                  pl.BlockSpec((1, seq, LANES), lambda b, g, i: (b, 0, groups + g)),
                  pl.BlockSpec((1, seq, LANES), lambda b, g, i: (b, 0, 2 * groups + g))],
        out_specs=pl.BlockSpec((1, ATTN_BLOCK, LANES), lambda b, g, i: (b, i, g)),
        out_shape=jax.ShapeDtypeStruct((n_batch, seq, SB_WIDTH), BF16),
        compiler_params=_params(3),
        name="sb_attention",
    )(proj, proj, proj)


def _diff_kernel(q_ref, k_ref, v_ref, lq1_ref, lk1_ref, lq2_ref, lk2_ref, g_ref, o_ref, *, lam_init):
    head = pl.program_id(1)
    qi = pl.program_id(2)
    blk = ATTN_BLOCK
    q = q_ref[0]
    row = lax.broadcasted_iota(jnp.int32, (blk, blk), 0)
    col = lax.broadcasted_iota(jnp.int32, (blk, blk), 1)
    causal = col <= row

    normed = []
    for comp in range(2):
        qc = jnp.where(_head_lanes(q.shape, comp), q, jnp.zeros_like(q))
        s = jnp.where(causal, _qk(qc, k_ref[0, pl.ds(qi * blk, blk), :]), -jnp.inf)
        m0 = jnp.full((blk, 1), -jnp.inf, F32)
        l0 = jnp.zeros((blk, 1), F32)
        acc0 = jnp.zeros((blk, LANES), F32)
        carry = _softmax_step(s, v_ref[0, pl.ds(qi * blk, blk), :], m0, l0, acc0)

        def body(j, carry, qc=qc):
            s = _qk(qc, k_ref[0, pl.ds(j * blk, blk), :])
            return _softmax_step(s, v_ref[0, pl.ds(j * blk, blk), :], *carry)

        m, l, acc = lax.fori_loop(0, qi, body, carry)
        normed.append(acc / l)

    def lam_term(a_ref, b_ref):
        prod = a_ref[pl.ds(head, 1), :] * b_ref[pl.ds(head, 1), :]
        return jnp.exp(jnp.sum(prod, axis=-1, keepdims=True))

    lam = lam_term(lq1_ref, lk1_ref) - lam_term(lq2_ref, lk2_ref) + lam_init
    o = normed[0] - lam * normed[1]
    o_ref[0] = (_rms(o, g_ref[...]) * (1.0 - lam_init)).astype(o_ref.dtype)


def _diff_attention(proj, n_batch, seq, lq1, lk1, lq2, lk2, subln, lam_init):
    q0 = 3 * SB_WIDTH // LANES
    k0 = q0 + DIFF_WIDTH // LANES
    v0 = k0 + DIFF_WIDTH // LANES
    lam_spec = _resident((DIFF_HEADS, HEAD_DIM), lambda b, h, i: (0, 0))
    return pl.pallas_call(
        functools.partial(_diff_kernel, lam_init=lam_init),
        grid=(n_batch, DIFF_HEADS, seq // ATTN_BLOCK),
        in_specs=[pl.BlockSpec((1, ATTN_BLOCK, LANES), lambda b, h, i: (b, i, q0 + h)),
                  pl.BlockSpec((1, seq, LANES), lambda b, h, i: (b, 0, k0 + h)),
                  pl.BlockSpec((1, seq, LANES), lambda b, h, i: (b, 0, v0 + h)),
                  lam_spec, lam_spec, lam_spec, lam_spec,
                  _resident((1, LANES), lambda b, h, i: (0, 0))],
        out_specs=pl.BlockSpec((1, ATTN_BLOCK, LANES), lambda b, h, i: (b, i, h)),
        out_shape=jax.ShapeDtypeStruct((n_batch, seq, DIFF_WIDTH), BF16),
        compiler_params=_params(3),
        name="diff_attention",
    )(proj, proj, proj, lq1, lk1, lq2, lk2, subln.reshape(1, LANES))


def _moba_kernel(q_ref, k_ref, v_ref, o_ref, kmean_ref, *, n_blocks):
    qi = pl.program_id(2)
    blk = MOBA_BLOCK

    @pl.when(qi == 0)
    def _():
        kmean_ref[...] = jnp.zeros_like(kmean_ref)

        def fill(j, carry):
            kb = k_ref[0, pl.ds(j * blk, blk), :].astype(F32)
            mean = jnp.sum(kb, axis=0, keepdims=True) * (1.0 / blk)
            kmean_ref[pl.ds(j, 1), :] = mean
            kmean_ref[pl.ds(j + HEAD_DIM, 1), :] = mean
            return carry

        lax.fori_loop(0, n_blocks, fill, 0)

    q = q_ref[0]
    km = kmean_ref[...]
    km_hi = km.astype(BF16)
    km_lo = (km - km_hi.astype(F32)).astype(BF16)
    row = lax.broadcasted_iota(jnp.int32, (blk, blk), 0)
    col = lax.broadcasted_iota(jnp.int32, (blk, blk), 1)
    causal = col <= row
    lane = lax.broadcasted_iota(jnp.int32, (blk, LANES), 1)
    lane_blk = lane % HEAD_DIM
    lane_f = lane.astype(F32)

    outs = []
    for head in range(2):
        own = _head_lanes(q.shape, head)
        spare = jnp.logical_not(own)
        qh = jnp.where(own, q, jnp.zeros_like(q))
        gate = _qk(qh, km_hi) + _qk(qh, km_lo)
        g = jnp.where(spare & (lane_blk < qi), gate, -jnp.inf)
        chosen = jnp.zeros(g.shape, jnp.bool_)
        for _ in range(MOBA_TOPK):
            top = jnp.max(g, axis=-1, keepdims=True)
            first = jnp.min(jnp.where(g == top, lane_f, float(2 * LANES)), axis=-1, keepdims=True)
            pick = (lane_f == first) & (top > -jnp.inf)
            chosen = chosen | pick
            g = jnp.where(pick, -jnp.inf, g)
        bias = jnp.where(chosen, 0.0, MASK_BIAS).astype(BF16)
        q_aug = jnp.where(own, q, bias)

        s = jnp.where(causal, _qk(qh, k_ref[0, pl.ds(qi * blk, blk), :]), -jnp.inf)
        m0 = jnp.full((blk, 1), -jnp.inf, F32)
        l0 = jnp.zeros((blk, 1), F32)
        acc0 = jnp.zeros((blk, LANES), F32)
        carry = _softmax_step(s, v_ref[0, pl.ds(qi * blk, blk), :], m0, l0, acc0)

        def body(j, carry, q_aug=q_aug, own=own):
            kj = k_ref[0, pl.ds(j * blk, blk), :]
            one_hot = jnp.where(lane_blk == j, 1.0, 0.0).astype(BF16)
            k_aug = jnp.where(own, kj, one_hot)
            return _softmax_step(_qk(q_aug, k_aug), v_ref[0, pl.ds(j * blk, blk), :], *carry)

        m, l, acc = lax.fori_loop(0, qi, body, carry)
        outs.append(acc / l)
    o_ref[0] = jnp.where(_head_lanes(outs[0].shape, 0), outs[0], outs[1]).astype(o_ref.dtype)


def _moba_attention(proj, n_batch, seq):
    groups = MOBA_WIDTH // LANES
    n_blocks = seq // MOBA_BLOCK
    assert seq % MOBA_BLOCK == 0 and n_blocks <= HEAD_DIM, "block choice rides on 64 spare lanes"
    return pl.pallas_call(
        functools.partial(_moba_kernel, n_blocks=n_blocks),
        grid=(n_batch, groups, n_blocks),
        in_specs=[pl.BlockSpec((1, MOBA_BLOCK, LANES), lambda b, g, i: (b, i, g)),
                  pl.BlockSpec((1, seq, LANES), lambda b, g, i: (b, 0, groups + g)),
                  pl.BlockSpec((1, seq, LANES), lambda b, g, i: (b, 0, 2 * groups + g))],
        out_specs=pl.BlockSpec((1, MOBA_BLOCK, LANES), lambda b, g, i: (b, i, g)),
        out_shape=jax.ShapeDtypeStruct((n_batch, seq, MOBA_WIDTH), BF16),
        scratch_shapes=[pltpu.VMEM((LANES, LANES), F32)],
        compiler_params=_params(3),
        name="moba_attention",
    )(proj, proj, proj)


def _out_proj_kernel(h_ref, a_ref, b_ref, wa_ref, wb_ref, o_ref):
    o_ref[...] = (h_ref[...]
                  + jnp.dot(a_ref[...], wa_ref[...], preferred_element_type=F32)
                  + jnp.dot(b_ref[...], wb_ref[...], preferred_element_type=F32))


def _out_proj(h, a, b, a_col, b_col, w):
    t, d = h.shape
    half = w.shape[0] // 2
    tm = min(TOKEN_TILE, t)
    return pl.pallas_call(
        _out_proj_kernel,
        grid=(t // tm,),
        in_specs=[pl.BlockSpec((tm, d), lambda i: (i, 0)),
                  pl.BlockSpec((tm, half), lambda i: (i, a_col)),
                  pl.BlockSpec((tm, half), lambda i: (i, b_col)),
                  _resident((half, d), lambda i: (0, 0)),
                  _resident((half, d), lambda i: (1, 0))],
        out_specs=pl.BlockSpec((tm, d), lambda i: (i, 0)),
        out_shape=jax.ShapeDtypeStruct((t, d), F32),
        compiler_params=_params(1),
        name="out_proj",
    )(h, a, b, w, w)


def _mlp_kernel(h_ref, g_ref, w1_ref, w2_ref, o_ref):
    h = h_ref[...]
    u = _rms(h, g_ref[...]).astype(BF16)
    o_ref[...] = h
    d_ff = w1_ref.shape[1]
    for c in range(d_ff // PROJ_CHUNK):
        cols = slice(c * PROJ_CHUNK, (c + 1) * PROJ_CHUNK)
        a = jnp.maximum(jnp.dot(u, w1_ref[:, cols], preferred_element_type=F32), 0.0)
        o_ref[...] += jnp.dot((a * a).astype(BF16), w2_ref[cols, :], preferred_element_type=F32)


def _mlp(h, g, w1, w2):
    t, d = h.shape
    d_ff = w1.shape[1]
    tm = min(TOKEN_TILE, t)
    return pl.pallas_call(
        _mlp_kernel,
        grid=(t // tm,),
        in_specs=[pl.BlockSpec((tm, d), lambda i: (i, 0)),
                  _resident((1, d), lambda i: (0, 0)),
                  _resident((d, d_ff), lambda i: (0, 0)),
                  _resident((d_ff, d), lambda i: (0, 0))],
        out_specs=pl.BlockSpec((tm, d), lambda i: (i, 0)),
        out_shape=jax.ShapeDtypeStruct((t, d), F32),
        compiler_params=_params(1),
        name="mlp",
    )(h, g.reshape(1, d), w1, w2)


def _ple_kernel(h_ref, p_ref, g_ref, wg_ref, wp_ref, gf_ref, o_ref, *, final_norm):
    h = h_ref[...]
    u = _rms(h, g_ref[...]).astype(BF16)
    gate = jax.nn.sigmoid(jnp.dot(u, wg_ref[...], preferred_element_type=F32))
    emb = jnp.dot(p_ref[...].astype(BF16), wp_ref[...], preferred_element_type=F32)
    out = h + gate * emb
    if final_norm:
        out = _rms(out, gf_ref[...])
    o_ref[...] = out


def _ple(h, p, g, wg, wp, g_final, final_norm):
    t, d = h.shape
    dp = p.shape[1]
    tm = min(TOKEN_TILE, t)
    return pl.pallas_call(
        functools.partial(_ple_kernel, final_norm=final_norm),
        grid=(t // tm,),
        in_specs=[pl.BlockSpec((tm, d), lambda i: (i, 0)),
                  pl.BlockSpec((tm, dp), lambda i: (i, 0)),
                  _resident((1, d), lambda i: (0, 0)),
                  _resident((d, d), lambda i: (0, 0)),
                  _resident((dp, d), lambda i: (0, 0)),
                  _resident((1, d), lambda i: (0, 0))],
        out_specs=pl.BlockSpec((tm, d), lambda i: (i, 0)),
        out_shape=jax.ShapeDtypeStruct((t, d), F32),
        compiler_params=_params(1),
        name="ple",
    )(h, p, g.reshape(1, d), wg, wp, g_final.reshape(1, d))


def kernel(x, p, positions, attn_norm, ab_w_in, ab_w_out, diff_lam_q1, diff_lam_k1, diff_lam_q2,
           diff_lam_k2, diff_subln, moba_w_in, moba_w_out, mlp_norm, w_ff1, w_ff2, ple_norm,
           ple_gate, ple_proj, final_norm):
    n_batch, seq, d = x.shape
    depth = p.shape[0]
    t = n_batch * seq
    assert seq % ATTN_BLOCK == 0 and t % TOKEN_TILE == 0
    cos_t, sin_t = _rope_tables(positions)
    h = x.reshape(t, d)
    even_rope, even_scale = (3, 4), (0, 3)
    odd_rope, odd_scale = (0, 1, 2, 3), (0, 1)
    for i in range(depth):
        j = i // 2
        if i % 2 == 0:
            proj = _norm_proj(h, attn_norm[i], ab_w_in[j].astype(BF16), cos_t, sin_t,
                              even_rope, even_scale).reshape(n_batch, seq, -1)
            lam_init = 0.8 - 0.6 * math.exp(-0.3 * i)
            o_a = _sb_attention(proj, n_batch, seq).reshape(t, SB_WIDTH)
            o_b = _diff_attention(proj, n_batch, seq, diff_lam_q1[j], diff_lam_k1[j], diff_lam_q2[j],
                                  diff_lam_k2[j], diff_subln[j], lam_init).reshape(t, DIFF_WIDTH)
            h = _out_proj(h, o_a, o_b, 0, 0, ab_w_out[j].astype(BF16))
        else:
            proj = _norm_proj(h, attn_norm[i], moba_w_in[j].astype(BF16), cos_t, sin_t,
                              odd_rope, odd_scale).reshape(n_batch, seq, -1)
            o = _moba_attention(proj, n_batch, seq).reshape(t, MOBA_WIDTH)
            h = _out_proj(h, o, o, 0, 1, moba_w_out[j].astype(BF16))
        h = _mlp(h, mlp_norm[i], w_ff1[i].astype(BF16), w_ff2[i].astype(BF16))
        h = _ple(h, p[i].reshape(t, -1), ple_norm[i], ple_gate[i].astype(BF16),
                 ple_proj[i].astype(BF16), final_norm, i == depth - 1)
    return h.reshape(n_batch, seq, d)
```

```python
import functools
import math

import jax
import jax.numpy as jnp
from jax import lax
from jax.experimental import pallas as pl
from jax.experimental.pallas import tpu as pltpu

F32 = jnp.float32
BF16 = jnp.bfloat16

NORM_EPS = 1e-6
ROPE_THETA = 500000.0
HEAD_DIM = 64
ROPE_DIMS = HEAD_DIM // 4
ROPE_HALF = ROPE_DIMS // 2
LANES = 128
QK_SCALE = HEAD_DIM ** -0.5
SB_WIDTH = 512
DIFF_WIDTH = 512
DIFF_HEADS = 4
MOBA_WIDTH = 1024
MOBA_BLOCK = 256
MOBA_TOPK = 3
MOBA_TILE = 512
ATTN_BLOCK = 256
DIFF_BLOCK = 512
PROJ_CHUNK = 512
TOKEN_TILE = 512
MASK_BIAS = -1e30
SB_CUTOFF = 110.0
VMEM_LIMIT = 48 * 1024 * 1024


def _params(n_axes):
    return pltpu.CompilerParams(dimension_semantics=("arbitrary",) * n_axes,
                                vmem_limit_bytes=VMEM_LIMIT)


def _resident(shape, index_map):
    return pl.BlockSpec(shape, index_map, pipeline_mode=pl.Buffered(1))


def _rope_table_kernel(pos_ref, invf_ref, cos_ref, sin_ref):
    ang = pos_ref[...].astype(F32) * invf_ref[...]
    lane = lax.broadcasted_iota(jnp.int32, ang.shape, 1) % HEAD_DIM
    c = jnp.cos(ang)
    s = jnp.sin(ang)
    cos_ref[...] = jnp.where(lane < ROPE_DIMS, c, 1.0)
    sin_ref[...] = jnp.where(lane < ROPE_HALF, -s, jnp.where(lane < ROPE_DIMS, s, 0.0))


def _rope_tables(positions):
    t = positions.size
    pos = positions.reshape(t, 1)
    inv_freq = ROPE_THETA ** (-(jnp.arange(0, ROPE_DIMS, 2, dtype=F32) / ROPE_DIMS))
    invf = jnp.tile(inv_freq, LANES // ROPE_HALF).reshape(1, LANES)
    tm = min(2048, t)
    return pl.pallas_call(
        _rope_table_kernel,
        grid=(t // tm,),
        in_specs=[pl.BlockSpec((tm, 1), lambda i: (i, 0)),
                  pl.BlockSpec((1, LANES), lambda i: (0, 0))],
        out_specs=[pl.BlockSpec((tm, LANES), lambda i: (i, 0))] * 2,
        out_shape=[jax.ShapeDtypeStruct((t, LANES), F32)] * 2,
        compiler_params=_params(1),
        name="rope_tables",
    )(pos, invf)


def _rope(a, cos, sin):
    lane = lax.broadcasted_iota(jnp.int32, a.shape, 1) % HEAD_DIM
    partner = jnp.where(lane < ROPE_HALF,
                        pltpu.roll(a, LANES - ROPE_HALF, 1),
                        pltpu.roll(a, ROPE_HALF, 1))
    return a * cos + partner * sin


def _rms(x, g):
    ms = jnp.mean(x * x, axis=-1, keepdims=True)
    return x * lax.rsqrt(ms + NORM_EPS) * g


def _norm_proj_kernel(x_ref, g_ref, w_ref, cos_ref, sin_ref, o_ref, *, rope_chunks, scale_chunks):
    u = _rms(x_ref[...], g_ref[...]).astype(BF16)
    n = w_ref.shape[1]
    for c in range(n // PROJ_CHUNK):
        cols = slice(c * PROJ_CHUNK, (c + 1) * PROJ_CHUNK)
        acc = jnp.dot(u, w_ref[:, cols], preferred_element_type=F32)
        if c in scale_chunks:
            acc = acc * QK_SCALE
        if c in rope_chunks:
            cos = cos_ref[...]
            sin = sin_ref[...]
            for gidx in range(PROJ_CHUNK // LANES):
                lanes = slice(gidx * LANES, (gidx + 1) * LANES)
                o_ref[:, c * PROJ_CHUNK + gidx * LANES:c * PROJ_CHUNK + (gidx + 1) * LANES] = (
                    _rope(acc[:, lanes], cos, sin).astype(BF16))
        else:
            o_ref[:, cols] = acc.astype(BF16)


def _norm_proj(h, g, w, cos_t, sin_t, rope_chunks, scale_chunks):
    t, d = h.shape
    n = w.shape[1]
    tm = min(TOKEN_TILE, t)
    kern = functools.partial(_norm_proj_kernel, rope_chunks=rope_chunks, scale_chunks=scale_chunks)
    return pl.pallas_call(
        kern,
        grid=(t // tm,),
        in_specs=[pl.BlockSpec((tm, d), lambda i: (i, 0)),
                  _resident((1, d), lambda i: (0, 0)),
                  _resident((d, n), lambda i: (0, 0)),
                  pl.BlockSpec((tm, LANES), lambda i: (i, 0)),
                  pl.BlockSpec((tm, LANES), lambda i: (i, 0))],
        out_specs=pl.BlockSpec((tm, n), lambda i: (i, 0)),
        out_shape=jax.ShapeDtypeStruct((t, n), BF16),
        compiler_params=_params(1),
        name="norm_proj",
    )(h, g.reshape(1, d), w, cos_t, sin_t)


def _head_lanes(shape, head):
    lane = lax.broadcasted_iota(jnp.int32, shape, 1)
    return (lane >= head * HEAD_DIM) & (lane < (head + 1) * HEAD_DIM)


def _qk(q, k):
    return lax.dot_general(q, k, (((1,), (1,)), ((), ())), preferred_element_type=F32)


def _softmax_step(s, v, m, l, acc):
    m_new = jnp.maximum(m, jnp.max(s, axis=-1, keepdims=True))
    alpha = jnp.exp(m - m_new)
    p = jnp.exp(s - m_new)
    l = alpha * l + jnp.sum(p, axis=-1, keepdims=True)
    acc = alpha * acc + jnp.dot(p.astype(BF16), v, preferred_element_type=F32)
    return m_new, l, acc


def _sb_kernel(q_ref, k_ref, v_ref, o_ref):
    qi = pl.program_id(2)
    blk = ATTN_BLOCK
    q = q_ref[0]
    q2 = jnp.concatenate([jnp.where(_head_lanes(q.shape, 0), q, jnp.zeros_like(q)),
                          jnp.where(_head_lanes(q.shape, 1), q, jnp.zeros_like(q))], axis=0)
    row = lax.broadcasted_iota(jnp.int32, (2 * blk, blk), 0) % blk
    col = lax.broadcasted_iota(jnp.int32, (2 * blk, blk), 1)
    past = col < row
    later = (lax.broadcasted_iota(jnp.int32, (blk, blk), 0)
             > lax.broadcasted_iota(jnp.int32, (blk, blk), 1)).astype(BF16)

    def block(j, c, acc, diag):
        kj = k_ref[0, pl.ds(j * blk, blk), :]
        vj = v_ref[0, pl.ds(j * blk, blk), :]
        z = _qk(q2, kj)
        log_keep_all = -(jnp.maximum(z, 0.0) + jnp.log(1.0 + jnp.exp(-jnp.abs(z))))
        log_keep = jnp.where(past, log_keep_all, 0.0) if diag else log_keep_all
        hi = log_keep.astype(BF16)
        lo = (log_keep - hi.astype(F32)).astype(BF16)
        tail = (jnp.dot(hi, later, preferred_element_type=F32)
                + jnp.dot(lo, later, preferred_element_type=F32) + c)
        w = jnp.exp(log_keep_all + z + tail)
        if diag:
            w = jnp.where(past, w, 0.0)
        acc = acc + jnp.dot(w.astype(BF16), vj, preferred_element_type=F32)
        c = c + jnp.sum(log_keep, axis=-1, keepdims=True)
        return c, acc

    c, acc = block(qi, jnp.zeros((2 * blk, 1), F32), jnp.zeros((2 * blk, LANES), F32), True)

    def more(carry):
        j, c, _ = carry
        return (j >= 0) & (jnp.max(c) > -SB_CUTOFF)

    def body(carry):
        j, c, acc = carry
        c, acc = block(j, c, acc, False)
        return j - 1, c, acc

    _, c, acc = lax.while_loop(more, body, (qi - 1, c, acc))
    o_ref[0] = jnp.where(_head_lanes((blk, LANES), 0), acc[:blk], acc[blk:]).astype(o_ref.dtype)


def _sb_attention(proj, n_batch, seq):
    groups = SB_WIDTH // LANES
    return pl.pallas_call(
        _sb_kernel,
        grid=(n_batch, groups, seq // ATTN_BLOCK),
        in_specs=[pl.BlockSpec((1, ATTN_BLOCK, LANES), lambda b, g, i: (b, i, g)),
                  pl.BlockSpec((1, seq, LANES), lambda b, g, i: (b, 0, groups + g)),
                  pl.BlockSpec((1, seq, LANES), lambda b, g, i: (b, 0, 2 * groups + g))],
        out_specs=pl.BlockSpec((1, ATTN_BLOCK, LANES), lambda b, g, i: (b, i, g)),
        out_shape=jax.ShapeDtypeStruct((n_batch, seq, SB_WIDTH), BF16),
        compiler_params=_params(3),
        name="sb_attention",
    )(proj, proj, proj)


def _diff_kernel(q_ref, k_ref, v_ref, lq1_ref, lk1_ref, lq2_ref, lk2_ref, g_ref, o_ref, *, lam_init):
    head = pl.program_id(1)
    qi = pl.program_id(2)
    blk = DIFF_BLOCK
    q = q_ref[0]
    q2 = jnp.concatenate([jnp.where(_head_lanes(q.shape, 0), q, jnp.zeros_like(q)),
                          jnp.where(_head_lanes(q.shape, 1), q, jnp.zeros_like(q))], axis=0)
    row = lax.broadcasted_iota(jnp.int32, (2 * blk, blk), 0) % blk
    col = lax.broadcasted_iota(jnp.int32, (2 * blk, blk), 1)

    s = jnp.where(col <= row, _qk(q2, k_ref[0, pl.ds(qi * blk, blk), :]), -jnp.inf)
    m0 = jnp.full((2 * blk, 1), -jnp.inf, F32)
    l0 = jnp.zeros((2 * blk, 1), F32)
    acc0 = jnp.zeros((2 * blk, LANES), F32)
    carry = _softmax_step(s, v_ref[0, pl.ds(qi * blk, blk), :], m0, l0, acc0)

    def body(j, carry):
        s = _qk(q2, k_ref[0, pl.ds(j * blk, blk), :])
        return _softmax_step(s, v_ref[0, pl.ds(j * blk, blk), :], *carry)

    m, l, acc = lax.fori_loop(0, qi, body, carry)
    normed = acc / l

    def lam_term(a_ref, b_ref):
        prod = a_ref[pl.ds(head, 1), :] * b_ref[pl.ds(head, 1), :]
        return jnp.exp(jnp.sum(prod, axis=-1, keepdims=True))

    lam = lam_term(lq1_ref, lk1_ref) - lam_term(lq2_ref, lk2_ref) + lam_init
    o = normed[:blk] - lam * normed[blk:]
    o_ref[0] = (_rms(o, g_ref[...]) * (1.0 - lam_init)).astype(o_ref.dtype)


def _diff_attention(proj, n_batch, seq, lq1, lk1, lq2, lk2, subln, lam_init):
    q0 = 3 * SB_WIDTH // LANES
    k0 = q0 + DIFF_WIDTH // LANES
    v0 = k0 + DIFF_WIDTH // LANES
    lam_spec = _resident((DIFF_HEADS, HEAD_DIM), lambda b, h, i: (0, 0))
    return pl.pallas_call(
        functools.partial(_diff_kernel, lam_init=lam_init),
        grid=(n_batch, DIFF_HEADS, seq // DIFF_BLOCK),
        in_specs=[pl.BlockSpec((1, DIFF_BLOCK, LANES), lambda b, h, i: (b, i, q0 + h)),
                  pl.BlockSpec((1, seq, LANES), lambda b, h, i: (b, 0, k0 + h)),
                  pl.BlockSpec((1, seq, LANES), lambda b, h, i: (b, 0, v0 + h)),
                  lam_spec, lam_spec, lam_spec, lam_spec,
                  _resident((1, LANES), lambda b, h, i: (0, 0))],
        out_specs=pl.BlockSpec((1, DIFF_BLOCK, LANES), lambda b, h, i: (b, i, h)),
        out_shape=jax.ShapeDtypeStruct((n_batch, seq, DIFF_WIDTH), BF16),
        compiler_params=_params(3),
        name="diff_attention",
    )(proj, proj, proj, lq1, lk1, lq2, lk2, subln.reshape(1, LANES))


def _moba_kernel(q_ref, k_ref, v_ref, o_ref, kmean_ref, kext_ref, *, n_blocks):
    qi = pl.program_id(2)
    tile = MOBA_TILE
    seq = k_ref.shape[1]

    @pl.when(qi == 0)
    def _():
        kmean_ref[...] = jnp.zeros_like(kmean_ref)

        def fill_mean(b, carry):
            kb = k_ref[0, pl.ds(b * MOBA_BLOCK, MOBA_BLOCK), :].astype(F32)
            kmean_ref[pl.ds(b, 1), :] = jnp.sum(kb, axis=0, keepdims=True) * (1.0 / MOBA_BLOCK)
            return carry

        lax.fori_loop(0, n_blocks, fill_mean, 0)

        def fill_ext(c, carry):
            rows = pl.ds(c * tile, tile)
            key_blk = (c * tile + lax.broadcasted_iota(jnp.int32, (tile, LANES), 0)) // MOBA_BLOCK
            lane = lax.broadcasted_iota(jnp.int32, (tile, LANES), 1)
            kext_ref[rows, :LANES] = k_ref[0, rows, :]
            kext_ref[rows, LANES:] = jnp.where(lane == key_blk, 1.0, 0.0).astype(BF16)
            return carry

        lax.fori_loop(0, seq // tile, fill_ext, 0)

    q = q_ref[0]
    q2 = jnp.concatenate([jnp.where(_head_lanes(q.shape, 0), q, jnp.zeros_like(q)),
                          jnp.where(_head_lanes(q.shape, 1), q, jnp.zeros_like(q))], axis=0)

    km = kmean_ref[...]
    km_hi = km.astype(BF16)
    km_lo = (km - km_hi.astype(F32)).astype(BF16)
    gate = _qk(q2, km_hi) + _qk(q2, km_lo)
    q_pos = lax.broadcasted_iota(jnp.int32, (2 * tile, LANES), 0) % tile + qi * tile
    own_blk = q_pos // MOBA_BLOCK
    lane = lax.broadcasted_iota(jnp.int32, (2 * tile, LANES), 1)
    lane_f = lane.astype(F32)
    g = jnp.where(lane < own_blk, gate, -jnp.inf)
    chosen = lane == own_blk
    for _ in range(MOBA_TOPK):
        top = jnp.max(g, axis=-1, keepdims=True)
        first = jnp.min(jnp.where(g == top, lane_f, float(2 * LANES)), axis=-1, keepdims=True)
        pick = (lane_f == first) & (top > -jnp.inf)
        chosen = chosen | pick
        g = jnp.where(pick, -jnp.inf, g)
    bias = jnp.where(chosen, 0.0, MASK_BIAS).astype(BF16)
    q_ext = jnp.concatenate([q2, bias], axis=1)

    row = lax.broadcasted_iota(jnp.int32, (2 * tile, tile), 0) % tile
    col = lax.broadcasted_iota(jnp.int32, (2 * tile, tile), 1)
    s = jnp.where(col <= row, _qk(q_ext, kext_ref[pl.ds(qi * tile, tile), :]), -jnp.inf)
    m0 = jnp.full((2 * tile, 1), -jnp.inf, F32)
    l0 = jnp.zeros((2 * tile, 1), F32)
    acc0 = jnp.zeros((2 * tile, LANES), F32)
    carry = _softmax_step(s, v_ref[0, pl.ds(qi * tile, tile), :], m0, l0, acc0)

    def body(j, carry):
        s = _qk(q_ext, kext_ref[pl.ds(j * tile, tile), :])
        return _softmax_step(s, v_ref[0, pl.ds(j * tile, tile), :], *carry)

    m, l, acc = lax.fori_loop(0, qi, body, carry)
    o = acc / l
    o_ref[0] = jnp.where(_head_lanes((tile, LANES), 0), o[:tile], o[tile:]).astype(o_ref.dtype)


def _moba_attention(proj, n_batch, seq):
    groups = MOBA_WIDTH // LANES
    n_blocks = seq // MOBA_BLOCK
    assert seq % MOBA_TILE == 0 and n_blocks <= LANES, "the block choice rides on 128 bias lanes"
    return pl.pallas_call(
        functools.partial(_moba_kernel, n_blocks=n_blocks),
        grid=(n_batch, groups, seq // MOBA_TILE),
        in_specs=[pl.BlockSpec((1, MOBA_TILE, LANES), lambda b, g, i: (b, i, g)),
                  pl.BlockSpec((1, seq, LANES), lambda b, g, i: (b, 0, groups + g)),
                  pl.BlockSpec((1, seq, LANES), lambda b, g, i: (b, 0, 2 * groups + g))],
        out_specs=pl.BlockSpec((1, MOBA_TILE, LANES), lambda b, g, i: (b, i, g)),
        out_shape=jax.ShapeDtypeStruct((n_batch, seq, MOBA_WIDTH), BF16),
        scratch_shapes=[pltpu.VMEM((LANES, LANES), F32),
                        pltpu.VMEM((seq, 2 * LANES), BF16)],
        compiler_params=_params(3),
        name="moba_attention",
    )(proj, proj, proj)


def _out_proj_kernel(h_ref, a_ref, b_ref, wa_ref, wb_ref, o_ref):
    o_ref[...] = (h_ref[...]
                  + jnp.dot(a_ref[...], wa_ref[...], preferred_element_type=F32)
                  + jnp.dot(b_ref[...], wb_ref[...], preferred_element_type=F32))


def _out_proj(h, a, b, a_col, b_col, w):
    t, d = h.shape
    half = w.shape[0] // 2
    tm = min(TOKEN_TILE, t)
    return pl.pallas_call(
        _out_proj_kernel,
        grid=(t // tm,),
        in_specs=[pl.BlockSpec((tm, d), lambda i: (i, 0)),
                  pl.BlockSpec((tm, half), lambda i: (i, a_col)),
                  pl.BlockSpec((tm, half), lambda i: (i, b_col)),
                  _resident((half, d), lambda i: (0, 0)),
                  _resident((half, d), lambda i: (1, 0))],
        out_specs=pl.BlockSpec((tm, d), lambda i: (i, 0)),
        out_shape=jax.ShapeDtypeStruct((t, d), F32),
        compiler_params=_params(1),
        name="out_proj",
    )(h, a, b, w, w)


def _mlp_kernel(h_ref, g_ref, w1_ref, w2_ref, o_ref):
    h = h_ref[...]
    u = _rms(h, g_ref[...]).astype(BF16)
    o_ref[...] = h
    d_ff = w1_ref.shape[1]
    for c in range(d_ff // PROJ_CHUNK):
        cols = slice(c * PROJ_CHUNK, (c + 1) * PROJ_CHUNK)
        a = jnp.maximum(jnp.dot(u, w1_ref[:, cols], preferred_element_type=F32), 0.0)
        o_ref[...] += jnp.dot((a * a).astype(BF16), w2_ref[cols, :], preferred_element_type=F32)


def _mlp(h, g, w1, w2):
    t, d = h.shape
    d_ff = w1.shape[1]
    tm = min(TOKEN_TILE, t)
    return pl.pallas_call(
        _mlp_kernel,
        grid=(t // tm,),
        in_specs=[pl.BlockSpec((tm, d), lambda i: (i, 0)),
                  _resident((1, d), lambda i: (0, 0)),
                  _resident((d, d_ff), lambda i: (0, 0)),
                  _resident((d_ff, d), lambda i: (0, 0))],
        out_specs=pl.BlockSpec((tm, d), lambda i: (i, 0)),
        out_shape=jax.ShapeDtypeStruct((t, d), F32),
        compiler_params=_params(1),
        name="mlp",
    )(h, g.reshape(1, d), w1, w2)


def _ple_kernel(h_ref, p_ref, g_ref, wg_ref, wp_ref, gf_ref, o_ref, *, final_norm):
    h = h_ref[...]
    u = _rms(h, g_ref[...]).astype(BF16)
    gate = jax.nn.sigmoid(jnp.dot(u, wg_ref[...], preferred_element_type=F32))
    emb = jnp.dot(p_ref[...].astype(BF16), wp_ref[...], preferred_element_type=F32)
    out = h + gate * emb
    if final_norm:
        out = _rms(out, gf_ref[...])
    o_ref[...] = out


def _ple(h, p, g, wg, wp, g_final, final_norm):
    t, d = h.shape
    dp = p.shape[1]
    tm = min(TOKEN_TILE, t)
    return pl.pallas_call(
        functools.partial(_ple_kernel, final_norm=final_norm),
        grid=(t // tm,),
        in_specs=[pl.BlockSpec((tm, d), lambda i: (i, 0)),
                  pl.BlockSpec((tm, dp), lambda i: (i, 0)),
                  _resident((1, d), lambda i: (0, 0)),
                  _resident((d, d), lambda i: (0, 0)),
                  _resident((dp, d), lambda i: (0, 0)),
                  _resident((1, d), lambda i: (0, 0))],
        out_specs=pl.BlockSpec((tm, d), lambda i: (i, 0)),
        out_shape=jax.ShapeDtypeStruct((t, d), F32),
        compiler_params=_params(1),
        name="ple",
    )(h, p, g.reshape(1, d), wg, wp, g_final.reshape(1, d))


def kernel(x, p, positions, attn_norm, ab_w_in, ab_w_out, diff_lam_q1, diff_lam_k1, diff_lam_q2,
           diff_lam_k2, diff_subln, moba_w_in, moba_w_out, mlp_norm, w_ff1, w_ff2, ple_norm,
           ple_gate, ple_proj, final_norm):
    n_batch, seq, d = x.shape
    depth = p.shape[0]
    t = n_batch * seq
    assert seq % DIFF_BLOCK == 0 and seq % ATTN_BLOCK == 0 and t % TOKEN_TILE == 0
    cos_t, sin_t = _rope_tables(positions)
    h = x.reshape(t, d)
    even_rope, even_scale = (3, 4), (0, 3)
    odd_rope, odd_scale = (0, 1, 2, 3), (0, 1)
    for i in range(depth):
        j = i // 2
        if i % 2 == 0:
            proj = _norm_proj(h, attn_norm[i], ab_w_in[j].astype(BF16), cos_t, sin_t,
                              even_rope, even_scale).reshape(n_batch, seq, -1)
            lam_init = 0.8 - 0.6 * math.exp(-0.3 * i)
            o_a = _sb_attention(proj, n_batch, seq).reshape(t, SB_WIDTH)
            o_b = _diff_attention(proj, n_batch, seq, diff_lam_q1[j], diff_lam_k1[j], diff_lam_q2[j],
                                  diff_lam_k2[j], diff_subln[j], lam_init).reshape(t, DIFF_WIDTH)
            h = _out_proj(h, o_a, o_b, 0, 0, ab_w_out[j].astype(BF16))
        else:
            proj = _norm_proj(h, attn_norm[i], moba_w_in[j].astype(BF16), cos_t, sin_t,
                              odd_rope, odd_scale).reshape(n_batch, seq, -1)
            o = _moba_attention(proj, n_batch, seq).reshape(t, MOBA_WIDTH)
            h = _out_proj(h, o, o, 0, 1, moba_w_out[j].astype(BF16))
        h = _mlp(h, mlp_norm[i], w_ff1[i].astype(BF16), w_ff2[i].astype(BF16))
        h = _ple(h, p[i].reshape(t, -1), ple_norm[i], ple_gate[i].astype(BF16),
                 ple_proj[i].astype(BF16), final_norm, i == depth - 1)
    return h.reshape(n_batch, seq, d)
```

```python
import functools
import math

import jax
import jax.numpy as jnp
from jax import lax
from jax.experimental import pallas as pl
from jax.experimental.pallas import tpu as pltpu

F32 = jnp.float32
BF16 = jnp.bfloat16

NORM_EPS = 1e-6
ROPE_THETA = 500000.0
HEAD_DIM = 64
ROPE_DIMS = HEAD_DIM // 4
ROPE_HALF = ROPE_DIMS // 2
LANES = 128
QK_SCALE = HEAD_DIM ** -0.5
SB_WIDTH = 512
DIFF_WIDTH = 512
DIFF_HEADS = 4
MOBA_WIDTH = 1024
MOBA_BLOCK = 256
MOBA_TOPK = 3
MOBA_TILE = 512
ATTN_BLOCK = 256
DIFF_BLOCK = 512
DIFF_CHUNK = 512
PROJ_CHUNK = 512
TOKEN_TILE = 512
MASK_BIAS = -1e30
SB_CUTOFF = 110.0
STALE_MAX_HEADROOM = 40.0
VMEM_LIMIT = 48 * 1024 * 1024


def _params(n_axes):
    return pltpu.CompilerParams(dimension_semantics=("arbitrary",) * n_axes,
                                vmem_limit_bytes=VMEM_LIMIT)


def _resident(shape, index_map):
    return pl.BlockSpec(shape, index_map, pipeline_mode=pl.Buffered(1))


def _rope_table_kernel(pos_ref, invf_ref, cos_ref, sin_ref):
    ang = pos_ref[...].astype(F32) * invf_ref[...]
    lane = lax.broadcasted_iota(jnp.int32, ang.shape, 1) % HEAD_DIM
    c = jnp.cos(ang)
    s = jnp.sin(ang)
    cos_ref[...] = jnp.where(lane < ROPE_DIMS, c, 1.0)
    sin_ref[...] = jnp.where(lane < ROPE_HALF, -s, jnp.where(lane < ROPE_DIMS, s, 0.0))


def _rope_tables(positions):
    t = positions.size
    pos = positions.reshape(t, 1)
    inv_freq = ROPE_THETA ** (-(jnp.arange(0, ROPE_DIMS, 2, dtype=F32) / ROPE_DIMS))
    invf = jnp.tile(inv_freq, LANES // ROPE_HALF).reshape(1, LANES)
    tm = min(2048, t)
    return pl.pallas_call(
        _rope_table_kernel,
        grid=(t // tm,),
        in_specs=[pl.BlockSpec((tm, 1), lambda i: (i, 0)),
                  pl.BlockSpec((1, LANES), lambda i: (0, 0))],
        out_specs=[pl.BlockSpec((tm, LANES), lambda i: (i, 0))] * 2,
        out_shape=[jax.ShapeDtypeStruct((t, LANES), F32)] * 2,
        compiler_params=_params(1),
        name="rope_tables",
    )(pos, invf)


def _rope(a, cos, sin):
    lane = lax.broadcasted_iota(jnp.int32, a.shape, 1) % HEAD_DIM
    partner = jnp.where(lane < ROPE_HALF,
                        pltpu.roll(a, LANES - ROPE_HALF, 1),
                        pltpu.roll(a, ROPE_HALF, 1))
    return a * cos + partner * sin


def _rms(x, g):
    ms = jnp.mean(x * x, axis=-1, keepdims=True)
    return x * lax.rsqrt(ms + NORM_EPS) * g


def _norm_proj_kernel(x_ref, g_ref, w_ref, cos_ref, sin_ref, o_ref, *, rope_chunks, scale_chunks):
    u = _rms(x_ref[...], g_ref[...]).astype(BF16)
    n = w_ref.shape[1]
    for c in range(n // PROJ_CHUNK):
        cols = slice(c * PROJ_CHUNK, (c + 1) * PROJ_CHUNK)
        acc = jnp.dot(u, w_ref[:, cols], preferred_element_type=F32)
        if c in scale_chunks:
            acc = acc * QK_SCALE
        if c in rope_chunks:
            cos = cos_ref[...]
            sin = sin_ref[...]
            for gidx in range(PROJ_CHUNK // LANES):
                lanes = slice(gidx * LANES, (gidx + 1) * LANES)
                o_ref[:, c * PROJ_CHUNK + gidx * LANES:c * PROJ_CHUNK + (gidx + 1) * LANES] = (
                    _rope(acc[:, lanes], cos, sin).astype(BF16))
        else:
            o_ref[:, cols] = acc.astype(BF16)


def _norm_proj(h, g, w, cos_t, sin_t, rope_chunks, scale_chunks):
    t, d = h.shape
    n = w.shape[1]
    tm = min(TOKEN_TILE, t)
    kern = functools.partial(_norm_proj_kernel, rope_chunks=rope_chunks, scale_chunks=scale_chunks)
    return pl.pallas_call(
        kern,
        grid=(t // tm,),
        in_specs=[pl.BlockSpec((tm, d), lambda i: (i, 0)),
                  _resident((1, d), lambda i: (0, 0)),
                  _resident((d, n), lambda i: (0, 0)),
                  pl.BlockSpec((tm, LANES), lambda i: (i, 0)),
                  pl.BlockSpec((tm, LANES), lambda i: (i, 0))],
        out_specs=pl.BlockSpec((tm, n), lambda i: (i, 0)),
        out_shape=jax.ShapeDtypeStruct((t, n), BF16),
        compiler_params=_params(1),
        name="norm_proj",
    )(h, g.reshape(1, d), w, cos_t, sin_t)


def _head_lanes(shape, head):
    lane = lax.broadcasted_iota(jnp.int32, shape, 1)
    return (lane >= head * HEAD_DIM) & (lane < (head + 1) * HEAD_DIM)


def _qk(q, k):
    return lax.dot_general(q, k, (((1,), (1,)), ((), ())), preferred_element_type=F32)


def _softmax_init(m_ref, acc_ref):
    m_ref[...] = jnp.full(m_ref.shape, -jnp.inf, F32)
    acc_ref[...] = jnp.zeros(acc_ref.shape, F32)


def _softmax_step(s, v, m_ref, acc_ref):
    m = m_ref[...]
    m_new = jnp.maximum(m, jnp.max(s, axis=-1, keepdims=True))
    alpha = jnp.exp(m - m_new)
    alpha = jnp.concatenate([alpha, alpha], axis=1)
    p = jnp.concatenate([jnp.exp(s[:, c * LANES:(c + 1) * LANES] - m_new)
                         for c in range(s.shape[1] // LANES)], axis=1).astype(BF16)
    v_ones = jnp.concatenate([v, jnp.ones_like(v)], axis=1)
    half = p.shape[0] // 2
    acc_ref[:half] = alpha[:half] * acc_ref[:half] + jnp.dot(p[:half], v_ones, preferred_element_type=F32)
    acc_ref[half:] = alpha[half:] * acc_ref[half:] + jnp.dot(p[half:], v_ones, preferred_element_type=F32)
    m_ref[...] = m_new


def _softmax_sweep(score_fn, v_fn, n_chunks, s_refs, m_ref, acc_ref):
    s0_ref, s1_ref = s_refs

    def step(s_ref, j):
        _softmax_step(s_ref[...], v_fn(j), m_ref, acc_ref)

    @pl.when(n_chunks > 0)
    def _():
        s0_ref[...] = score_fn(0)

    def body(i, carry):
        s1_ref[...] = score_fn(2 * i + 1)
        step(s0_ref, 2 * i)
        s0_ref[...] = score_fn(2 * i + 2)
        step(s1_ref, 2 * i + 1)
        return carry

    n_pairs = (n_chunks - 1) // 2
    lax.fori_loop(0, n_pairs, body, 0)
    last = 2 * n_pairs

    @pl.when((n_chunks > 0) & (n_chunks % 2 == 0))
    def _():
        s1_ref[...] = score_fn(last + 1)
        step(s0_ref, last)
        step(s1_ref, last + 1)

    @pl.when(n_chunks % 2 == 1)
    def _():
        step(s0_ref, last)


def _softmax_result(acc_ref):
    return acc_ref[:, :LANES] / acc_ref[:, LANES:]


def _sb_kernel(q_ref, k_ref, v_ref, o_ref):
    qi = pl.program_id(2)
    blk = ATTN_BLOCK
    q = q_ref[0]
    q2 = jnp.concatenate([jnp.where(_head_lanes(q.shape, 0), q, jnp.zeros_like(q)),
                          jnp.where(_head_lanes(q.shape, 1), q, jnp.zeros_like(q))], axis=0)
    row = lax.broadcasted_iota(jnp.int32, (2 * blk, blk), 0) % blk
    col = lax.broadcasted_iota(jnp.int32, (2 * blk, blk), 1)
    past = col < row
    later = (lax.broadcasted_iota(jnp.int32, (blk, blk), 0)
             > lax.broadcasted_iota(jnp.int32, (blk, blk), 1)).astype(BF16)

    def block(j, c, acc, diag):
        kj = k_ref[0, pl.ds(j * blk, blk), :]
        vj = v_ref[0, pl.ds(j * blk, blk), :]
        z = _qk(q2, kj)
        log_keep_all = -(jnp.maximum(z, 0.0) + jnp.log(1.0 + jnp.exp(-jnp.abs(z))))
        log_keep = jnp.where(past, log_keep_all, 0.0) if diag else log_keep_all
        hi = log_keep.astype(BF16)
        lo = (log_keep - hi.astype(F32)).astype(BF16)
        tail = (jnp.dot(hi, later, preferred_element_type=F32)
                + jnp.dot(lo, later, preferred_element_type=F32) + c)
        w = jnp.exp(log_keep_all + z + tail)
        if diag:
            w = jnp.where(past, w, 0.0)
        acc = acc + jnp.dot(w.astype(BF16), vj, preferred_element_type=F32)
        c = c + jnp.sum(log_keep, axis=-1, keepdims=True)
        return c, acc

    c, acc = block(qi, jnp.zeros((2 * blk, 1), F32), jnp.zeros((2 * blk, LANES), F32), True)

    def more(carry):
        j, c, _ = carry
        return (j >= 0) & (jnp.max(c) > -SB_CUTOFF)

    def body(carry):
        j, c, acc = carry
        c, acc = block(j, c, acc, False)
        return j - 1, c, acc

    _, c, acc = lax.while_loop(more, body, (qi - 1, c, acc))
    o_ref[0] = jnp.where(_head_lanes((blk, LANES), 0), acc[:blk], acc[blk:]).astype(o_ref.dtype)


def _sb_attention(proj, n_batch, seq):
    groups = SB_WIDTH // LANES
    return pl.pallas_call(
        _sb_kernel,
        grid=(n_batch, groups, seq // ATTN_BLOCK),
        in_specs=[pl.BlockSpec((1, ATTN_BLOCK, LANES), lambda b, g, i: (b, i, g)),
                  pl.BlockSpec((1, seq, LANES), lambda b, g, i: (b, 0, groups + g)),
                  pl.BlockSpec((1, seq, LANES), lambda b, g, i: (b, 0, 2 * groups + g))],
        out_specs=pl.BlockSpec((1, ATTN_BLOCK, LANES), lambda b, g, i: (b, i, g)),
        out_shape=jax.ShapeDtypeStruct((n_batch, seq, SB_WIDTH), BF16),
        compiler_params=_params(3),
        name="sb_attention",
    )(proj, proj, proj)


def _diff_kernel(q_ref, k_ref, v_ref, lq1_ref, lk1_ref, lq2_ref, lk2_ref, g_ref, o_ref,
                 s0_ref, s1_ref, m_ref, acc_ref, *, lam_init):
    head = pl.program_id(1)
    qi = pl.program_id(2)
    blk = DIFF_BLOCK
    tk = DIFF_CHUNK
    q = q_ref[0]
    q2 = jnp.concatenate([jnp.where(_head_lanes(q.shape, 0), q, jnp.zeros_like(q)),
                          jnp.where(_head_lanes(q.shape, 1), q, jnp.zeros_like(q))], axis=0)
    row = lax.broadcasted_iota(jnp.int32, (2 * blk, tk), 0) % blk
    col = lax.broadcasted_iota(jnp.int32, (2 * blk, tk), 1)

    _softmax_init(m_ref, acc_ref)
    for c in range(blk // tk):
        keys = pl.ds(qi * blk + c * tk, tk)
        s = jnp.where(col + c * tk <= row, _qk(q2, k_ref[0, keys, :]), -jnp.inf)
        _softmax_step(s, v_ref[0, keys, :], m_ref, acc_ref)

    _softmax_sweep(lambda j: _qk(q2, k_ref[0, pl.ds(j * tk, tk), :]),
                   lambda j: v_ref[0, pl.ds(j * tk, tk), :],
                   qi * (blk // tk), (s0_ref, s1_ref), m_ref, acc_ref)
    normed = _softmax_result(acc_ref)

    def lam_term(a_ref, b_ref):
        prod = a_ref[pl.ds(head, 1), :] * b_ref[pl.ds(head, 1), :]
        return jnp.exp(jnp.sum(prod, axis=-1, keepdims=True))

    lam = lam_term(lq1_ref, lk1_ref) - lam_term(lq2_ref, lk2_ref) + lam_init
    o = normed[:blk] - lam * normed[blk:]
    o_ref[0] = (_rms(o, g_ref[...]) * (1.0 - lam_init)).astype(o_ref.dtype)


def _diff_attention(proj, n_batch, seq, lq1, lk1, lq2, lk2, subln, lam_init):
    q0 = 3 * SB_WIDTH // LANES
    k0 = q0 + DIFF_WIDTH // LANES
    v0 = k0 + DIFF_WIDTH // LANES
    lam_spec = _resident((DIFF_HEADS, HEAD_DIM), lambda b, h, i: (0, 0))
    return pl.pallas_call(
        functools.partial(_diff_kernel, lam_init=lam_init),
        grid=(n_batch, DIFF_HEADS, seq // DIFF_BLOCK),
        in_specs=[pl.BlockSpec((1, DIFF_BLOCK, LANES), lambda b, h, i: (b, i, q0 + h)),
                  pl.BlockSpec((1, seq, LANES), lambda b, h, i: (b, 0, k0 + h)),
                  pl.BlockSpec((1, seq, LANES), lambda b, h, i: (b, 0, v0 + h)),
                  lam_spec, lam_spec, lam_spec, lam_spec,
                  _resident((1, LANES), lambda b, h, i: (0, 0))],
        out_specs=pl.BlockSpec((1, DIFF_BLOCK, LANES), lambda b, h, i: (b, i, h)),
        out_shape=jax.ShapeDtypeStruct((n_batch, seq, DIFF_WIDTH), BF16),
        scratch_shapes=[pltpu.VMEM((2 * DIFF_BLOCK, DIFF_CHUNK), F32),
                        pltpu.VMEM((2 * DIFF_BLOCK, DIFF_CHUNK), F32),
                        pltpu.VMEM((2 * DIFF_BLOCK, LANES), F32),
                        pltpu.VMEM((2 * DIFF_BLOCK, 2 * LANES), F32)],
        compiler_params=_params(3),
        name="diff_attention",
    )(proj, proj, proj, lq1, lk1, lq2, lk2, subln.reshape(1, LANES))


def _moba_kernel(q_ref, k_ref, v_ref, o_ref, kmean_ref, kext_ref, s0_ref, s1_ref, m_ref, acc_ref, *,
                 n_blocks):
    qi = pl.program_id(2)
    tile = MOBA_TILE
    seq = k_ref.shape[1]

    @pl.when(qi == 0)
    def _():
        kmean_ref[...] = jnp.zeros_like(kmean_ref)

        def fill_mean(b, carry):
            kb = k_ref[0, pl.ds(b * MOBA_BLOCK, MOBA_BLOCK), :].astype(F32)
            kmean_ref[pl.ds(b, 1), :] = jnp.sum(kb, axis=0, keepdims=True) * (1.0 / MOBA_BLOCK)
            return carry

        lax.fori_loop(0, n_blocks, fill_mean, 0)

        def fill_ext(c, carry):
            rows = pl.ds(c * tile, tile)
            key_blk = (c * tile + lax.broadcasted_iota(jnp.int32, (tile, LANES), 0)) // MOBA_BLOCK
            lane = lax.broadcasted_iota(jnp.int32, (tile, LANES), 1)
            kext_ref[rows, :LANES] = k_ref[0, rows, :]
            kext_ref[rows, LANES:] = jnp.where(lane == key_blk, 1.0, 0.0).astype(BF16)
            return carry

        lax.fori_loop(0, seq // tile, fill_ext, 0)

    q = q_ref[0]
    q2 = jnp.concatenate([jnp.where(_head_lanes(q.shape, 0), q, jnp.zeros_like(q)),
                          jnp.where(_head_lanes(q.shape, 1), q, jnp.zeros_like(q))], axis=0)

    km = kmean_ref[...]
    km_hi = km.astype(BF16)
    km_lo = (km - km_hi.astype(F32)).astype(BF16)
    gate = _qk(q2, km_hi) + _qk(q2, km_lo)
    q_pos = lax.broadcasted_iota(jnp.int32, (2 * tile, LANES), 0) % tile + qi * tile
    own_blk = q_pos // MOBA_BLOCK
    lane = lax.broadcasted_iota(jnp.int32, (2 * tile, LANES), 1)
    lane_f = lane.astype(F32)
    g = jnp.where(lane < own_blk, gate, -jnp.inf)
    chosen = lane == own_blk
    for _ in range(MOBA_TOPK):
        top = jnp.max(g, axis=-1, keepdims=True)
        first = jnp.min(jnp.where(g == top, lane_f, float(2 * LANES)), axis=-1, keepdims=True)
        pick = (lane_f == first) & (top > -jnp.inf)
        chosen = chosen | pick
        g = jnp.where(pick, -jnp.inf, g)
    bias = jnp.where(chosen, 0.0, MASK_BIAS).astype(BF16)
    q_ext = jnp.concatenate([q2, bias], axis=1)

    row = lax.broadcasted_iota(jnp.int32, (2 * tile, tile), 0) % tile
    col = lax.broadcasted_iota(jnp.int32, (2 * tile, tile), 1)
    s = jnp.where(col <= row, _qk(q_ext, kext_ref[pl.ds(qi * tile, tile), :]), -jnp.inf)
    _softmax_init(m_ref, acc_ref)
    _softmax_step(s, v_ref[0, pl.ds(qi * tile, tile), :], m_ref, acc_ref)

    _softmax_sweep(lambda j: _qk(q_ext, kext_ref[pl.ds(j * tile, tile), :]),
                   lambda j: v_ref[0, pl.ds(j * tile, tile), :],
                   qi, (s0_ref, s1_ref), m_ref, acc_ref)
    o = _softmax_result(acc_ref)
    o_ref[0] = jnp.where(_head_lanes((tile, LANES), 0), o[:tile], o[tile:]).astype(o_ref.dtype)


def _moba_attention(proj, n_batch, seq):
    groups = MOBA_WIDTH // LANES
    n_blocks = seq // MOBA_BLOCK
    assert seq % MOBA_TILE == 0 and n_blocks <= LANES, "the block choice rides on 128 bias lanes"
    return pl.pallas_call(
        functools.partial(_moba_kernel, n_blocks=n_blocks),
        grid=(n_batch, groups, seq // MOBA_TILE),
        in_specs=[pl.BlockSpec((1, MOBA_TILE, LANES), lambda b, g, i: (b, i, g)),
                  pl.BlockSpec((1, seq, LANES), lambda b, g, i: (b, 0, groups + g)),
                  pl.BlockSpec((1, seq, LANES), lambda b, g, i: (b, 0, 2 * groups + g))],
        out_specs=pl.BlockSpec((1, MOBA_TILE, LANES), lambda b, g, i: (b, i, g)),
        out_shape=jax.ShapeDtypeStruct((n_batch, seq, MOBA_WIDTH), BF16),
        scratch_shapes=[pltpu.VMEM((LANES, LANES), F32),
                        pltpu.VMEM((seq, 2 * LANES), BF16),
                        pltpu.VMEM((2 * MOBA_TILE, MOBA_TILE), F32),
                        pltpu.VMEM((2 * MOBA_TILE, MOBA_TILE), F32),
                        pltpu.VMEM((2 * MOBA_TILE, LANES), F32),
                        pltpu.VMEM((2 * MOBA_TILE, 2 * LANES), F32)],
        compiler_params=_params(3),
        name="moba_attention",
    )(proj, proj, proj)


def _out_proj_kernel(h_ref, a_ref, b_ref, wa_ref, wb_ref, o_ref):
    o_ref[...] = (h_ref[...]
                  + jnp.dot(a_ref[...], wa_ref[...], preferred_element_type=F32)
                  + jnp.dot(b_ref[...], wb_ref[...], preferred_element_type=F32))


def _out_proj(h, a, b, a_col, b_col, w):
    t, d = h.shape
    half = w.shape[0] // 2
    tm = min(TOKEN_TILE, t)
    return pl.pallas_call(
        _out_proj_kernel,
        grid=(t // tm,),
        in_specs=[pl.BlockSpec((tm, d), lambda i: (i, 0)),
                  pl.BlockSpec((tm, half), lambda i: (i, a_col)),
                  pl.BlockSpec((tm, half), lambda i: (i, b_col)),
                  _resident((half, d), lambda i: (0, 0)),
                  _resident((half, d), lambda i: (1, 0))],
        out_specs=pl.BlockSpec((tm, d), lambda i: (i, 0)),
        out_shape=jax.ShapeDtypeStruct((t, d), F32),
        compiler_params=_params(1),
        name="out_proj",
    )(h, a, b, w, w)


def _mlp_kernel(h_ref, g_ref, w1_ref, w2_ref, o_ref):
    h = h_ref[...]
    u = _rms(h, g_ref[...]).astype(BF16)
    o_ref[...] = h
    d_ff = w1_ref.shape[1]
    for c in range(d_ff // PROJ_CHUNK):
        cols = slice(c * PROJ_CHUNK, (c + 1) * PROJ_CHUNK)
        a = jnp.maximum(jnp.dot(u, w1_ref[:, cols], preferred_element_type=F32), 0.0)
        o_ref[...] += jnp.dot((a * a).astype(BF16), w2_ref[cols, :], preferred_element_type=F32)


def _mlp(h, g, w1, w2):
    t, d = h.shape
    d_ff = w1.shape[1]
    tm = min(TOKEN_TILE, t)
    return pl.pallas_call(
        _mlp_kernel,
        grid=(t // tm,),
        in_specs=[pl.BlockSpec((tm, d), lambda i: (i, 0)),
                  _resident((1, d), lambda i: (0, 0)),
                  _resident((d, d_ff), lambda i: (0, 0)),
                  _resident((d_ff, d), lambda i: (0, 0))],
        out_specs=pl.BlockSpec((tm, d), lambda i: (i, 0)),
        out_shape=jax.ShapeDtypeStruct((t, d), F32),
        compiler_params=_params(1),
        name="mlp",
    )(h, g.reshape(1, d), w1, w2)


def _ple_kernel(h_ref, p_ref, g_ref, wg_ref, wp_ref, gf_ref, o_ref, *, final_norm):
    h = h_ref[...]
    u = _rms(h, g_ref[...]).astype(BF16)
    gate = jax.nn.sigmoid(jnp.dot(u, wg_ref[...], preferred_element_type=F32))
    emb = jnp.dot(p_ref[...].astype(BF16), wp_ref[...], preferred_element_type=F32)
    out = h + gate * emb
    if final_norm:
        out = _rms(out, gf_ref[...])
    o_ref[...] = out


def _ple(h, p, g, wg, wp, g_final, final_norm):
    t, d = h.shape
    dp = p.shape[1]
    tm = min(TOKEN_TILE, t)
    return pl.pallas_call(
        functools.partial(_ple_kernel, final_norm=final_norm),
        grid=(t // tm,),
        in_specs=[pl.BlockSpec((tm, d), lambda i: (i, 0)),
                  pl.BlockSpec((tm, dp), lambda i: (i, 0)),
                  _resident((1, d), lambda i: (0, 0)),
                  _resident((d, d), lambda i: (0, 0)),
                  _resident((dp, d), lambda i: (0, 0)),
                  _resident((1, d), lambda i: (0, 0))],
        out_specs=pl.BlockSpec((tm, d), lambda i: (i, 0)),
        out_shape=jax.ShapeDtypeStruct((t, d), F32),
        compiler_params=_params(1),
        name="ple",
    )(h, p, g.reshape(1, d), wg, wp, g_final.reshape(1, d))


def kernel(x, p, positions, attn_norm, ab_w_in, ab_w_out, diff_lam_q1, diff_lam_k1, diff_lam_q2,
           diff_lam_k2, diff_subln, moba_w_in, moba_w_out, mlp_norm, w_ff1, w_ff2, ple_norm,
           ple_gate, ple_proj, final_norm):
    n_batch, seq, d = x.shape
    depth = p.shape[0]
    t = n_batch * seq
    assert seq % DIFF_BLOCK == 0 and seq % ATTN_BLOCK == 0 and t % TOKEN_TILE == 0
    cos_t, sin_t = _rope_tables(positions)
    h = x.reshape(t, d)
    even_rope, even_scale = (3, 4), (0, 3)
    odd_rope, odd_scale = (0, 1, 2, 3), (0, 1)
    for i in range(depth):
        j = i // 2
        if i % 2 == 0:
            proj = _norm_proj(h, attn_norm[i], ab_w_in[j].astype(BF16), cos_t, sin_t,
                              even_rope, even_scale).reshape(n_batch, seq, -1)
            lam_init = 0.8 - 0.6 * math.exp(-0.3 * i)
            o_a = _sb_attention(proj, n_batch, seq).reshape(t, SB_WIDTH)
            o_b = _diff_attention(proj, n_batch, seq, diff_lam_q1[j], diff_lam_k1[j], diff_lam_q2[j],
                                  diff_lam_k2[j], diff_subln[j], lam_init).reshape(t, DIFF_WIDTH)
            h = _out_proj(h, o_a, o_b, 0, 0, ab_w_out[j].astype(BF16))
        else:
            proj = _norm_proj(h, attn_norm[i], moba_w_in[j].astype(BF16), cos_t, sin_t,
                              odd_rope, odd_scale).reshape(n_batch, seq, -1)
            o = _moba_attention(proj, n_batch, seq).reshape(t, MOBA_WIDTH)
            h = _out_proj(h, o, o, 0, 1, moba_w_out[j].astype(BF16))
        h = _mlp(h, mlp_norm[i], w_ff1[i].astype(BF16), w_ff2[i].astype(BF16))
        h = _ple(h, p[i].reshape(t, -1), ple_norm[i], ple_gate[i].astype(BF16),
                 ple_proj[i].astype(BF16), final_norm, i == depth - 1)
    return h.reshape(n_batch, seq, d)
```

```python
import functools
import math

import jax
import jax.numpy as jnp
from jax import lax
from jax.experimental import pallas as pl
from jax.experimental.pallas import tpu as pltpu

F32 = jnp.float32
BF16 = jnp.bfloat16

NORM_EPS = 1e-6
ROPE_THETA = 500000.0
HEAD_DIM = 64
ROPE_DIMS = HEAD_DIM // 4
ROPE_HALF = ROPE_DIMS // 2
LANES = 128
QK_SCALE = HEAD_DIM ** -0.5
SB_WIDTH = 512
DIFF_WIDTH = 512
DIFF_HEADS = 4
MOBA_WIDTH = 1024
MOBA_BLOCK = 256
MOBA_TOPK = 3
MOBA_TILE = 512
ATTN_BLOCK = 256
DIFF_BLOCK = 512
DIFF_CHUNK = 512
PROJ_CHUNK = 512
TOKEN_TILE = 512
MASK_BIAS = -1e30
SB_CUTOFF = 110.0
VMEM_LIMIT = 48 * 1024 * 1024


def _params(n_axes):
    return pltpu.CompilerParams(dimension_semantics=("arbitrary",) * n_axes,
                                vmem_limit_bytes=VMEM_LIMIT)


def _resident(shape, index_map):
    return pl.BlockSpec(shape, index_map, pipeline_mode=pl.Buffered(1))


def _rope_table_kernel(pos_ref, invf_ref, cos_ref, sin_ref):
    ang = pos_ref[...].astype(F32) * invf_ref[...]
    lane = lax.broadcasted_iota(jnp.int32, ang.shape, 1) % HEAD_DIM
    c = jnp.cos(ang)
    s = jnp.sin(ang)
    cos_ref[...] = jnp.where(lane < ROPE_DIMS, c, 1.0)
    sin_ref[...] = jnp.where(lane < ROPE_HALF, -s, jnp.where(lane < ROPE_DIMS, s, 0.0))


def _rope_tables(positions):
    t = positions.size
    pos = positions.reshape(t, 1)
    inv_freq = ROPE_THETA ** (-(jnp.arange(0, ROPE_DIMS, 2, dtype=F32) / ROPE_DIMS))
    invf = jnp.tile(inv_freq, LANES // ROPE_HALF).reshape(1, LANES)
    tm = min(2048, t)
    return pl.pallas_call(
        _rope_table_kernel,
        grid=(t // tm,),
        in_specs=[pl.BlockSpec((tm, 1), lambda i: (i, 0)),
                  pl.BlockSpec((1, LANES), lambda i: (0, 0))],
        out_specs=[pl.BlockSpec((tm, LANES), lambda i: (i, 0))] * 2,
        out_shape=[jax.ShapeDtypeStruct((t, LANES), F32)] * 2,
        compiler_params=_params(1),
        name="rope_tables",
    )(pos, invf)


def _rope(a, cos, sin):
    lane = lax.broadcasted_iota(jnp.int32, a.shape, 1) % HEAD_DIM
    partner = jnp.where(lane < ROPE_HALF,
                        pltpu.roll(a, LANES - ROPE_HALF, 1),
                        pltpu.roll(a, ROPE_HALF, 1))
    return a * cos + partner * sin


def _rms(x, g):
    ms = jnp.mean(x * x, axis=-1, keepdims=True)
    return x * lax.rsqrt(ms + NORM_EPS) * g


def _norm_proj_kernel(x_ref, g_ref, w_ref, cos_ref, sin_ref, o_ref, *, rope_chunks, scale_chunks):
    u = _rms(x_ref[...], g_ref[...]).astype(BF16)
    n = w_ref.shape[1]
    for c in range(n // PROJ_CHUNK):
        cols = slice(c * PROJ_CHUNK, (c + 1) * PROJ_CHUNK)
        acc = jnp.dot(u, w_ref[:, cols], preferred_element_type=F32)
        if c in scale_chunks:
            acc = acc * QK_SCALE
        if c in rope_chunks:
            cos = cos_ref[...]
            sin = sin_ref[...]
            for gidx in range(PROJ_CHUNK // LANES):
                lanes = slice(gidx * LANES, (gidx + 1) * LANES)
                o_ref[:, c * PROJ_CHUNK + gidx * LANES:c * PROJ_CHUNK + (gidx + 1) * LANES] = (
                    _rope(acc[:, lanes], cos, sin).astype(BF16))
        else:
            o_ref[:, cols] = acc.astype(BF16)


def _norm_proj(h, g, w, cos_t, sin_t, rope_chunks, scale_chunks):
    t, d = h.shape
    n = w.shape[1]
    tm = min(TOKEN_TILE, t)
    kern = functools.partial(_norm_proj_kernel, rope_chunks=rope_chunks, scale_chunks=scale_chunks)
    return pl.pallas_call(
        kern,
        grid=(t // tm,),
        in_specs=[pl.BlockSpec((tm, d), lambda i: (i, 0)),
                  _resident((1, d), lambda i: (0, 0)),
                  _resident((d, n), lambda i: (0, 0)),
                  pl.BlockSpec((tm, LANES), lambda i: (i, 0)),
                  pl.BlockSpec((tm, LANES), lambda i: (i, 0))],
        out_specs=pl.BlockSpec((tm, n), lambda i: (i, 0)),
        out_shape=jax.ShapeDtypeStruct((t, n), BF16),
        compiler_params=_params(1),
        name="norm_proj",
    )(h, g.reshape(1, d), w, cos_t, sin_t)


def _head_lanes(shape, head):
    lane = lax.broadcasted_iota(jnp.int32, shape, 1)
    return (lane >= head * HEAD_DIM) & (lane < (head + 1) * HEAD_DIM)


def _stack_heads(q):
    return jnp.concatenate([jnp.where(_head_lanes(q.shape, 0), q, jnp.zeros_like(q)),
                            jnp.where(_head_lanes(q.shape, 1), q, jnp.zeros_like(q))], axis=0)


def _qk(q, k):
    return lax.dot_general(q, k, (((1,), (1,)), ((), ())), preferred_element_type=F32)


def _softmax_init(m_ref, acc_ref):
    m_ref[...] = jnp.full(m_ref.shape, -jnp.inf, F32)
    acc_ref[...] = jnp.zeros(acc_ref.shape, F32)


def _softmax_step(s, v, m_ref, acc_ref):
    m = m_ref[...]
    m_new = jnp.maximum(m, jnp.max(s, axis=-1, keepdims=True))
    alpha = jnp.exp(m - m_new)
    alpha = jnp.concatenate([alpha, alpha], axis=1)
    p = jnp.concatenate([jnp.exp(s[:, c * LANES:(c + 1) * LANES] - m_new)
                         for c in range(s.shape[1] // LANES)], axis=1).astype(BF16)
    v_ones = jnp.concatenate([v, jnp.ones_like(v)], axis=1)
    half = p.shape[0] // 2
    acc_ref[:half] = alpha[:half] * acc_ref[:half] + jnp.dot(p[:half], v_ones, preferred_element_type=F32)
    acc_ref[half:] = alpha[half:] * acc_ref[half:] + jnp.dot(p[half:], v_ones, preferred_element_type=F32)
    m_ref[...] = m_new


def _softmax_sweep(score_fn, v_fn, n_chunks, s_refs, m_ref, acc_ref):
    s0_ref, s1_ref = s_refs

    def step(s_ref, j):
        _softmax_step(s_ref[...], v_fn(j), m_ref, acc_ref)

    @pl.when(n_chunks > 0)
    def _():
        s0_ref[...] = score_fn(0)

    def body(i, carry):
        s1_ref[...] = score_fn(2 * i + 1)
        step(s0_ref, 2 * i)
        s0_ref[...] = score_fn(2 * i + 2)
        step(s1_ref, 2 * i + 1)
        return carry

    n_pairs = (n_chunks - 1) // 2
    lax.fori_loop(0, n_pairs, body, 0)
    last = 2 * n_pairs

    @pl.when((n_chunks > 0) & (n_chunks % 2 == 0))
    def _():
        s1_ref[...] = score_fn(last + 1)
        step(s0_ref, last)
        step(s1_ref, last + 1)

    @pl.when(n_chunks % 2 == 1)
    def _():
        step(s0_ref, last)


def _softmax_result(acc_ref):
    return acc_ref[:, :LANES] / acc_ref[:, LANES:]


def _sb_kernel(q_ref, k_ref, v_ref, o_ref):
    qi = pl.program_id(2)
    blk = ATTN_BLOCK
    q2 = _stack_heads(q_ref[0])
    row = lax.broadcasted_iota(jnp.int32, (2 * blk, blk), 0) % blk
    col = lax.broadcasted_iota(jnp.int32, (2 * blk, blk), 1)
    past = col < row
    later = (lax.broadcasted_iota(jnp.int32, (blk, blk), 0)
             > lax.broadcasted_iota(jnp.int32, (blk, blk), 1)).astype(BF16)

    def scores(j):
        z = _qk(q2, k_ref[0, pl.ds(j * blk, blk), :])
        return z, -(jnp.maximum(z, 0.0) + jnp.log(1.0 + jnp.exp(-jnp.abs(z))))

    def tail_in_block(log_keep):
        hi = log_keep.astype(BF16)
        lo = (log_keep - hi.astype(F32)).astype(BF16)
        return (jnp.dot(hi, later, preferred_element_type=F32)
                + jnp.dot(lo, later, preferred_element_type=F32))

    def weighted_values(j, w):
        return jnp.dot(w.astype(BF16), v_ref[0, pl.ds(j * blk, blk), :], preferred_element_type=F32)

    prev = jnp.maximum(qi - 1, 0)
    z_d, log_keep_d_all = scores(qi)
    z_p, log_keep_p = scores(prev)
    log_keep_d = jnp.where(past, log_keep_d_all, 0.0)
    c_d = jnp.sum(log_keep_d, axis=-1, keepdims=True)
    w_d = jnp.where(past, jnp.exp(log_keep_d_all + z_d + tail_in_block(log_keep_d)), 0.0)
    w_p = jnp.exp(log_keep_p + z_p + (tail_in_block(log_keep_p) + c_d))
    w_p = jnp.where(qi > 0, w_p, 0.0)
    acc = weighted_values(qi, w_d) + weighted_values(prev, w_p)
    c = c_d + jnp.sum(log_keep_p, axis=-1, keepdims=True)

    def more(carry):
        j, c, _ = carry
        return (j >= 0) & (jnp.max(c) > -SB_CUTOFF)

    def body(carry):
        j, c, acc = carry
        z, log_keep = scores(j)
        w = jnp.exp(log_keep + z + (tail_in_block(log_keep) + c))
        return j - 1, c + jnp.sum(log_keep, axis=-1, keepdims=True), acc + weighted_values(j, w)

    _, c, acc = lax.while_loop(more, body, (qi - 2, c, acc))
    o_ref[0] = jnp.where(_head_lanes((blk, LANES), 0), acc[:blk], acc[blk:]).astype(o_ref.dtype)


def _sb_attention(proj, n_batch, seq):
    groups = SB_WIDTH // LANES
    return pl.pallas_call(
        _sb_kernel,
        grid=(n_batch, groups, seq // ATTN_BLOCK),
        in_specs=[pl.BlockSpec((1, ATTN_BLOCK, LANES), lambda b, g, i: (b, i, g)),
                  pl.BlockSpec((1, seq, LANES), lambda b, g, i: (b, 0, groups + g)),
                  pl.BlockSpec((1, seq, LANES), lambda b, g, i: (b, 0, 2 * groups + g))],
        out_specs=pl.BlockSpec((1, ATTN_BLOCK, LANES), lambda b, g, i: (b, i, g)),
        out_shape=jax.ShapeDtypeStruct((n_batch, seq, SB_WIDTH), BF16),
        compiler_params=_params(3),
        name="sb_attention",
    )(proj, proj, proj)


def _diff_kernel(q_ref, k_ref, v_ref, lq1_ref, lk1_ref, lq2_ref, lk2_ref, g_ref, o_ref,
                 s0_ref, s1_ref, m_ref, acc_ref, *, lam_init):
    head = pl.program_id(1)
    qi = pl.program_id(2)
    blk = DIFF_BLOCK
    tk = DIFF_CHUNK
    q2 = _stack_heads(q_ref[0])
    row = lax.broadcasted_iota(jnp.int32, (2 * blk, tk), 0) % blk
    col = lax.broadcasted_iota(jnp.int32, (2 * blk, tk), 1)

    _softmax_init(m_ref, acc_ref)
    for c in range(blk // tk):
        keys = pl.ds(qi * blk + c * tk, tk)
        s = jnp.where(col + c * tk <= row, _qk(q2, k_ref[0, keys, :]), -jnp.inf)
        _softmax_step(s, v_ref[0, keys, :], m_ref, acc_ref)

    _softmax_sweep(lambda j: _qk(q2, k_ref[0, pl.ds(j * tk, tk), :]),
                   lambda j: v_ref[0, pl.ds(j * tk, tk), :],
                   qi * (blk // tk), (s0_ref, s1_ref), m_ref, acc_ref)
    normed = _softmax_result(acc_ref)

    def lam_term(a_ref, b_ref):
        prod = a_ref[pl.ds(head, 1), :] * b_ref[pl.ds(head, 1), :]
        return jnp.exp(jnp.sum(prod, axis=-1, keepdims=True))

    lam = lam_term(lq1_ref, lk1_ref) - lam_term(lq2_ref, lk2_ref) + lam_init
    o = normed[:blk] - lam * normed[blk:]
    o_ref[0] = (_rms(o, g_ref[...]) * (1.0 - lam_init)).astype(o_ref.dtype)


def _diff_attention(proj, n_batch, seq, lq1, lk1, lq2, lk2, subln, lam_init):
    q0 = 3 * SB_WIDTH // LANES
    k0 = q0 + DIFF_WIDTH // LANES
    v0 = k0 + DIFF_WIDTH // LANES
    lam_spec = _resident((DIFF_HEADS, HEAD_DIM), lambda b, h, i: (0, 0))
    return pl.pallas_call(
        functools.partial(_diff_kernel, lam_init=lam_init),
        grid=(n_batch, DIFF_HEADS, seq // DIFF_BLOCK),
        in_specs=[pl.BlockSpec((1, DIFF_BLOCK, LANES), lambda b, h, i: (b, i, q0 + h)),
                  pl.BlockSpec((1, seq, LANES), lambda b, h, i: (b, 0, k0 + h)),
                  pl.BlockSpec((1, seq, LANES), lambda b, h, i: (b, 0, v0 + h)),
                  lam_spec, lam_spec, lam_spec, lam_spec,
                  _resident((1, LANES), lambda b, h, i: (0, 0))],
        out_specs=pl.BlockSpec((1, DIFF_BLOCK, LANES), lambda b, h, i: (b, i, h)),
        out_shape=jax.ShapeDtypeStruct((n_batch, seq, DIFF_WIDTH), BF16),
        scratch_shapes=[pltpu.VMEM((2 * DIFF_BLOCK, DIFF_CHUNK), F32),
                        pltpu.VMEM((2 * DIFF_BLOCK, DIFF_CHUNK), F32),
                        pltpu.VMEM((2 * DIFF_BLOCK, LANES), F32),
                        pltpu.VMEM((2 * DIFF_BLOCK, 2 * LANES), F32)],
        compiler_params=_params(3),
        name="diff_attention",
    )(proj, proj, proj, lq1, lk1, lq2, lk2, subln.reshape(1, LANES))


def _moba_kernel(q_ref, k_ref, v_ref, o_ref, kmean_ref, kext_ref, s0_ref, s1_ref, m_ref, acc_ref, *,
                 n_blocks):
    qi = pl.program_id(2)
    tile = MOBA_TILE
    seq = k_ref.shape[1]

    @pl.when(qi == 0)
    def _():
        kmean_ref[...] = jnp.zeros_like(kmean_ref)

        def fill_mean(b, carry):
            kb = k_ref[0, pl.ds(b * MOBA_BLOCK, MOBA_BLOCK), :].astype(F32)
            kmean_ref[pl.ds(b, 1), :] = jnp.sum(kb, axis=0, keepdims=True) * (1.0 / MOBA_BLOCK)
            return carry

        lax.fori_loop(0, n_blocks, fill_mean, 0)

        def fill_ext(c, carry):
            rows = pl.ds(c * tile, tile)
            key_blk = (c * tile + lax.broadcasted_iota(jnp.int32, (tile, LANES), 0)) // MOBA_BLOCK
            lane = lax.broadcasted_iota(jnp.int32, (tile, LANES), 1)
            kext_ref[rows, :LANES] = k_ref[0, rows, :]
            kext_ref[rows, LANES:] = jnp.where(lane == key_blk, 1.0, 0.0).astype(BF16)
            return carry

        lax.fori_loop(0, seq // tile, fill_ext, 0)

    q2 = _stack_heads(q_ref[0])

    km = kmean_ref[...]
    km_hi = km.astype(BF16)
    km_lo = (km - km_hi.astype(F32)).astype(BF16)
    gate = _qk(km_hi, q2) + _qk(km_lo, q2)
    q_pos = lax.broadcasted_iota(jnp.int32, (LANES, 2 * tile), 1) % tile + qi * tile
    own_blk = q_pos // MOBA_BLOCK
    blk_id = lax.broadcasted_iota(jnp.int32, (LANES, 2 * tile), 0)
    blk_f = blk_id.astype(F32)
    g = jnp.where(blk_id < own_blk, gate, -jnp.inf)
    chosen = blk_id == own_blk
    for _ in range(MOBA_TOPK):
        top = jnp.max(g, axis=0, keepdims=True)
        first = jnp.min(jnp.where(g == top, blk_f, float(2 * LANES)), axis=0, keepdims=True)
        pick = (blk_f == first) & (top > -jnp.inf)
        chosen = chosen | pick
        g = jnp.where(pick, -jnp.inf, g)
    bias = jnp.where(chosen, 0.0, MASK_BIAS).T.astype(BF16)
    q_ext = jnp.concatenate([q2, bias], axis=1)

    row = lax.broadcasted_iota(jnp.int32, (2 * tile, tile), 0) % tile
    col = lax.broadcasted_iota(jnp.int32, (2 * tile, tile), 1)
    s = jnp.where(col <= row, _qk(q_ext, kext_ref[pl.ds(qi * tile, tile), :]), -jnp.inf)
    _softmax_init(m_ref, acc_ref)
    _softmax_step(s, v_ref[0, pl.ds(qi * tile, tile), :], m_ref, acc_ref)

    _softmax_sweep(lambda j: _qk(q_ext, kext_ref[pl.ds(j * tile, tile), :]),
                   lambda j: v_ref[0, pl.ds(j * tile, tile), :],
                   qi, (s0_ref, s1_ref), m_ref, acc_ref)
    o = _softmax_result(acc_ref)
    o_ref[0] = jnp.where(_head_lanes((tile, LANES), 0), o[:tile], o[tile:]).astype(o_ref.dtype)


def _moba_attention(proj, n_batch, seq):
    groups = MOBA_WIDTH // LANES
    n_blocks = seq // MOBA_BLOCK
    assert seq % MOBA_TILE == 0 and n_blocks <= LANES, "the block choice rides on 128 bias lanes"
    return pl.pallas_call(
        functools.partial(_moba_kernel, n_blocks=n_blocks),
        grid=(n_batch, groups, seq // MOBA_TILE),
        in_specs=[pl.BlockSpec((1, MOBA_TILE, LANES), lambda b, g, i: (b, i, g)),
                  pl.BlockSpec((1, seq, LANES), lambda b, g, i: (b, 0, groups + g)),
                  pl.BlockSpec((1, seq, LANES), lambda b, g, i: (b, 0, 2 * groups + g))],
        out_specs=pl.BlockSpec((1, MOBA_TILE, LANES), lambda b, g, i: (b, i, g)),
        out_shape=jax.ShapeDtypeStruct((n_batch, seq, MOBA_WIDTH), BF16),
        scratch_shapes=[pltpu.VMEM((LANES, LANES), F32),
                        pltpu.VMEM((seq, 2 * LANES), BF16),
                        pltpu.VMEM((2 * MOBA_TILE, MOBA_TILE), F32),
                        pltpu.VMEM((2 * MOBA_TILE, MOBA_TILE), F32),
                        pltpu.VMEM((2 * MOBA_TILE, LANES), F32),
                        pltpu.VMEM((2 * MOBA_TILE, 2 * LANES), F32)],
        compiler_params=_params(3),
        name="moba_attention",
    )(proj, proj, proj)


def _post_attention_kernel(h_ref, a_ref, b_ref, p_ref, wa_ref, wb_ref, g_mlp_ref, w1_ref, w2_ref,
                           g_ple_ref, wg_ref, wp_ref, g_final_ref, o_ref, *, final_norm):
    h = (h_ref[...]
         + jnp.dot(a_ref[...], wa_ref[...], preferred_element_type=F32)
         + jnp.dot(b_ref[...], wb_ref[...], preferred_element_type=F32))
    u = _rms(h, g_mlp_ref[...]).astype(BF16)
    o_ref[...] = h
    for c in range(w1_ref.shape[1] // PROJ_CHUNK):
        cols = slice(c * PROJ_CHUNK, (c + 1) * PROJ_CHUNK)
        a = jnp.maximum(jnp.dot(u, w1_ref[:, cols], preferred_element_type=F32), 0.0)
        o_ref[...] += jnp.dot((a * a).astype(BF16), w2_ref[cols, :], preferred_element_type=F32)
    h = o_ref[...]
    u = _rms(h, g_ple_ref[...]).astype(BF16)
    gate = jax.nn.sigmoid(jnp.dot(u, wg_ref[...], preferred_element_type=F32))
    emb = jnp.dot(p_ref[...].astype(BF16), wp_ref[...], preferred_element_type=F32)
    out = h + gate * emb
    if final_norm:
        out = _rms(out, g_final_ref[...])
    o_ref[...] = out


def _post_attention(h, a, b, a_col, b_col, p, w_out, g_mlp, w1, w2, g_ple, wg, wp, g_final, final_norm):
    t, d = h.shape
    half = w_out.shape[0] // 2
    dp = p.shape[1]
    d_ff = w1.shape[1]
    tm = min(TOKEN_TILE, t)
    row = lambda i: (i, 0)
    fixed = lambda i: (0, 0)
    return pl.pallas_call(
        functools.partial(_post_attention_kernel, final_norm=final_norm),
        grid=(t // tm,),
        in_specs=[pl.BlockSpec((tm, d), row),
                  pl.BlockSpec((tm, half), lambda i: (i, a_col)),
                  pl.BlockSpec((tm, half), lambda i: (i, b_col)),
                  pl.BlockSpec((tm, dp), row),
                  _resident((half, d), fixed),
                  _resident((half, d), lambda i: (1, 0)),
                  _resident((1, d), fixed),
                  _resident((d, d_ff), fixed),
                  _resident((d_ff, d), fixed),
                  _resident((1, d), fixed),
                  _resident((d, d), fixed),
                  _resident((dp, d), fixed),
                  _resident((1, d), fixed)],
        out_specs=pl.BlockSpec((tm, d), row),
        out_shape=jax.ShapeDtypeStruct((t, d), F32),
        compiler_params=_params(1),
        name="post_attention",
    )(h, a, b, p, w_out, w_out, g_mlp.reshape(1, d), w1, w2, g_ple.reshape(1, d), wg, wp,
      g_final.reshape(1, d))


def kernel(x, p, positions, attn_norm, ab_w_in, ab_w_out, diff_lam_q1, diff_lam_k1, diff_lam_q2,
           diff_lam_k2, diff_subln, moba_w_in, moba_w_out, mlp_norm, w_ff1, w_ff2, ple_norm,
           ple_gate, ple_proj, final_norm):
    n_batch, seq, d = x.shape
    depth = p.shape[0]
    t = n_batch * seq
    assert seq % DIFF_BLOCK == 0 and seq % ATTN_BLOCK == 0 and t % TOKEN_TILE == 0
    cos_t, sin_t = _rope_tables(positions)
    h = x.reshape(t, d)
    even_rope, even_scale = (3, 4), (0, 3)
    odd_rope, odd_scale = (0, 1, 2, 3), (0, 1)
    for i in range(depth):
        j = i // 2
        if i % 2 == 0:
            proj = _norm_proj(h, attn_norm[i], ab_w_in[j].astype(BF16), cos_t, sin_t,
                              even_rope, even_scale).reshape(n_batch, seq, -1)
            lam_init = 0.8 - 0.6 * math.exp(-0.3 * i)
            o_a = _sb_attention(proj, n_batch, seq).reshape(t, SB_WIDTH)
            o_b = _diff_attention(proj, n_batch, seq, diff_lam_q1[j], diff_lam_k1[j], diff_lam_q2[j],
                                  diff_lam_k2[j], diff_subln[j], lam_init).reshape(t, DIFF_WIDTH)
            mixed, w_out = (o_a, o_b, 0, 0), ab_w_out[j]
        else:
            proj = _norm_proj(h, attn_norm[i], moba_w_in[j].astype(BF16), cos_t, sin_t,
                              odd_rope, odd_scale).reshape(n_batch, seq, -1)
            o = _moba_attention(proj, n_batch, seq).reshape(t, MOBA_WIDTH)
            mixed, w_out = (o, o, 0, 1), moba_w_out[j]
        h = _post_attention(h, *mixed, p[i].reshape(t, -1), w_out.astype(BF16), mlp_norm[i],
                            w_ff1[i].astype(BF16), w_ff2[i].astype(BF16), ple_norm[i],
                            ple_gate[i].astype(BF16), ple_proj[i].astype(BF16), final_norm,
                            i == depth - 1)
    return h.reshape(n_batch, seq, d)
```

```python
import functools
import math

import jax
import jax.numpy as jnp
from jax import lax
from jax.experimental import pallas as pl
from jax.experimental.pallas import tpu as pltpu

F32 = jnp.float32
BF16 = jnp.bfloat16

NORM_EPS = 1e-6
ROPE_THETA = 500000.0
HEAD_DIM = 64
ROPE_DIMS = HEAD_DIM // 4
ROPE_HALF = ROPE_DIMS // 2
LANES = 128
QK_SCALE = HEAD_DIM ** -0.5
SB_WIDTH = 512
DIFF_WIDTH = 512
DIFF_HEADS = 4
MOBA_WIDTH = 1024
MOBA_BLOCK = 256
MOBA_TOPK = 3
MOBA_TILE = 512
ATTN_BLOCK = 256
DIFF_BLOCK = 512
DIFF_CHUNK = 512
SWEEP_UNROLL = 4
PROJ_CHUNK = 512
TOKEN_TILE = 512
MASK_BIAS = -1e30
SB_CUTOFF = 110.0
VMEM_LIMIT = 48 * 1024 * 1024


def _params(n_axes):
    return pltpu.CompilerParams(dimension_semantics=("arbitrary",) * n_axes,
                                vmem_limit_bytes=VMEM_LIMIT)


def _resident(shape, index_map):
    return pl.BlockSpec(shape, index_map, pipeline_mode=pl.Buffered(1))


def _rope_table_kernel(pos_ref, invf_ref, cos_ref, sin_ref):
    ang = pos_ref[...].astype(F32) * invf_ref[...]
    lane = lax.broadcasted_iota(jnp.int32, ang.shape, 1) % HEAD_DIM
    c = jnp.cos(ang)
    s = jnp.sin(ang)
    cos_ref[...] = jnp.where(lane < ROPE_DIMS, c, 1.0)
    sin_ref[...] = jnp.where(lane < ROPE_HALF, -s, jnp.where(lane < ROPE_DIMS, s, 0.0))


def _rope_tables(positions):
    t = positions.size
    pos = positions.reshape(t, 1)
    inv_freq = ROPE_THETA ** (-(jnp.arange(0, ROPE_DIMS, 2, dtype=F32) / ROPE_DIMS))
    invf = jnp.tile(inv_freq, LANES // ROPE_HALF).reshape(1, LANES)
    tm = min(2048, t)
    return pl.pallas_call(
        _rope_table_kernel,
        grid=(t // tm,),
        in_specs=[pl.BlockSpec((tm, 1), lambda i: (i, 0)),
                  pl.BlockSpec((1, LANES), lambda i: (0, 0))],
        out_specs=[pl.BlockSpec((tm, LANES), lambda i: (i, 0))] * 2,
        out_shape=[jax.ShapeDtypeStruct((t, LANES), F32)] * 2,
        compiler_params=_params(1),
        name="rope_tables",
    )(pos, invf)


def _rope(a, cos, sin):
    lane = lax.broadcasted_iota(jnp.int32, a.shape, 1) % HEAD_DIM
    partner = jnp.where(lane < ROPE_HALF,
                        pltpu.roll(a, LANES - ROPE_HALF, 1),
                        pltpu.roll(a, ROPE_HALF, 1))
    return a * cos + partner * sin


def _rms(x, g):
    ms = jnp.mean(x * x, axis=-1, keepdims=True)
    return x * lax.rsqrt(ms + NORM_EPS) * g


def _norm_proj_kernel(x_ref, g_ref, w_ref, cos_ref, sin_ref, o_ref, *, rope_chunks, scale_chunks):
    u = _rms(x_ref[...], g_ref[...]).astype(BF16)
    n = w_ref.shape[1]
    for c in range(n // PROJ_CHUNK):
        cols = slice(c * PROJ_CHUNK, (c + 1) * PROJ_CHUNK)
        acc = jnp.dot(u, w_ref[:, cols], preferred_element_type=F32)
        if c in scale_chunks:
            acc = acc * QK_SCALE
        if c in rope_chunks:
            cos = cos_ref[...]
            sin = sin_ref[...]
            for gidx in range(PROJ_CHUNK // LANES):
                lanes = slice(gidx * LANES, (gidx + 1) * LANES)
                o_ref[:, c * PROJ_CHUNK + gidx * LANES:c * PROJ_CHUNK + (gidx + 1) * LANES] = (
                    _rope(acc[:, lanes], cos, sin).astype(BF16))
        else:
            o_ref[:, cols] = acc.astype(BF16)


def _norm_proj(h, g, w, cos_t, sin_t, rope_chunks, scale_chunks):
    t, d = h.shape
    n = w.shape[1]
    tm = min(TOKEN_TILE, t)
    kern = functools.partial(_norm_proj_kernel, rope_chunks=rope_chunks, scale_chunks=scale_chunks)
    return pl.pallas_call(
        kern,
        grid=(t // tm,),
        in_specs=[pl.BlockSpec((tm, d), lambda i: (i, 0)),
                  _resident((1, d), lambda i: (0, 0)),
                  _resident((d, n), lambda i: (0, 0)),
                  pl.BlockSpec((tm, LANES), lambda i: (i, 0)),
                  pl.BlockSpec((tm, LANES), lambda i: (i, 0))],
        out_specs=pl.BlockSpec((tm, n), lambda i: (i, 0)),
        out_shape=jax.ShapeDtypeStruct((t, n), BF16),
        compiler_params=_params(1),
        name="norm_proj",
    )(h, g.reshape(1, d), w, cos_t, sin_t)


def _head_lanes(shape, head):
    lane = lax.broadcasted_iota(jnp.int32, shape, 1)
    return (lane >= head * HEAD_DIM) & (lane < (head + 1) * HEAD_DIM)


def _stack_heads(q):
    return jnp.concatenate([jnp.where(_head_lanes(q.shape, 0), q, jnp.zeros_like(q)),
                            jnp.where(_head_lanes(q.shape, 1), q, jnp.zeros_like(q))], axis=0)


def _qk(q, k):
    return lax.dot_general(q, k, (((1,), (1,)), ((), ())), preferred_element_type=F32)


def _softmax_init(m_ref, acc_ref):
    m_ref[...] = jnp.full(m_ref.shape, -jnp.inf, F32)
    acc_ref[...] = jnp.zeros(acc_ref.shape, F32)


def _softmax_step(s, v, m_ref, acc_ref):
    m = m_ref[...]
    m_new = jnp.maximum(m, jnp.max(s, axis=-1, keepdims=True))
    alpha = jnp.exp(m - m_new)
    alpha = jnp.concatenate([alpha, alpha], axis=1)
    p = jnp.concatenate([jnp.exp(s[:, c * LANES:(c + 1) * LANES] - m_new)
                         for c in range(s.shape[1] // LANES)], axis=1).astype(BF16)
    v_ones = jnp.concatenate([v, jnp.ones_like(v)], axis=1)
    half = p.shape[0] // 2
    acc_ref[:half] = alpha[:half] * acc_ref[:half] + jnp.dot(p[:half], v_ones, preferred_element_type=F32)
    acc_ref[half:] = alpha[half:] * acc_ref[half:] + jnp.dot(p[half:], v_ones, preferred_element_type=F32)
    m_ref[...] = m_new


def _softmax_sweep(score_fn, v_fn, n_chunks, s_refs, m_ref, acc_ref):
    unroll = SWEEP_UNROLL

    def run(first, count, then_prefetch):
        for u in range(count):
            if u + 1 < count or then_prefetch:
                s_refs[(u + 1) % 2][...] = score_fn(first + u + 1)
            _softmax_step(s_refs[u % 2][...], v_fn(first + u), m_ref, acc_ref)

    def body(i, carry):
        run(unroll * i, unroll, True)
        return carry

    n_iters = jnp.maximum((n_chunks - 1) // unroll, 0)
    lax.fori_loop(0, n_iters, body, 0)
    last = unroll * n_iters

    for remaining in range(1, unroll + 1):
        @pl.when((n_chunks > 0) & (n_chunks - last == remaining))
        def _(remaining=remaining):
            run(last, remaining, False)


def _sweep_prefetch(score_fn, s_refs):
    s_refs[0][...] = score_fn(0)


def _softmax_result(acc_ref):
    return acc_ref[:, :LANES] / acc_ref[:, LANES:]


def _sb_kernel(q_ref, k_ref, v_ref, o_ref):
    qi = pl.program_id(2)
    blk = ATTN_BLOCK
    q2 = _stack_heads(q_ref[0])
    row = lax.broadcasted_iota(jnp.int32, (2 * blk, blk), 0) % blk
    col = lax.broadcasted_iota(jnp.int32, (2 * blk, blk), 1)
    past = col < row
    later = (lax.broadcasted_iota(jnp.int32, (blk, blk), 0)
             > lax.broadcasted_iota(jnp.int32, (blk, blk), 1)).astype(BF16)

    def scores(j):
        z = _qk(q2, k_ref[0, pl.ds(j * blk, blk), :])
        return z, -(jnp.maximum(z, 0.0) + jnp.log(1.0 + jnp.exp(-jnp.abs(z))))

    def tail_in_block(log_keep):
        hi = log_keep.astype(BF16)
        lo = (log_keep - hi.astype(F32)).astype(BF16)
        return (jnp.dot(hi, later, preferred_element_type=F32)
                + jnp.dot(lo, later, preferred_element_type=F32))

    def weighted_values(j, w):
        return jnp.dot(w.astype(BF16), v_ref[0, pl.ds(j * blk, blk), :], preferred_element_type=F32)

    prev = jnp.maximum(qi - 1, 0)
    z_d, log_keep_d_all = scores(qi)
    z_p, log_keep_p = scores(prev)
    log_keep_d = jnp.where(past, log_keep_d_all, 0.0)
    c_d = jnp.sum(log_keep_d, axis=-1, keepdims=True)
    w_d = jnp.where(past, jnp.exp(log_keep_d_all + z_d + tail_in_block(log_keep_d)), 0.0)
    w_p = jnp.exp(log_keep_p + z_p + (tail_in_block(log_keep_p) + c_d))
    w_p = jnp.where(qi > 0, w_p, 0.0)
    acc = weighted_values(qi, w_d) + weighted_values(prev, w_p)
    c = c_d + jnp.sum(log_keep_p, axis=-1, keepdims=True)

    def more(carry):
        j, c, _ = carry
        return (j >= 0) & (jnp.max(c) > -SB_CUTOFF)

    def body(carry):
        j, c, acc = carry
        z, log_keep = scores(j)
        w = jnp.exp(log_keep + z + (tail_in_block(log_keep) + c))
        return j - 1, c + jnp.sum(log_keep, axis=-1, keepdims=True), acc + weighted_values(j, w)

    _, c, acc = lax.while_loop(more, body, (qi - 2, c, acc))
    o_ref[0] = jnp.where(_head_lanes((blk, LANES), 0), acc[:blk], acc[blk:]).astype(o_ref.dtype)


def _sb_attention(proj, n_batch, seq):
    groups = SB_WIDTH // LANES
    return pl.pallas_call(
        _sb_kernel,
        grid=(n_batch, groups, seq // ATTN_BLOCK),
        in_specs=[pl.BlockSpec((1, ATTN_BLOCK, LANES), lambda b, g, i: (b, i, g)),
                  pl.BlockSpec((1, seq, LANES), lambda b, g, i: (b, 0, groups + g)),
                  pl.BlockSpec((1, seq, LANES), lambda b, g, i: (b, 0, 2 * groups + g))],
        out_specs=pl.BlockSpec((1, ATTN_BLOCK, LANES), lambda b, g, i: (b, i, g)),
        out_shape=jax.ShapeDtypeStruct((n_batch, seq, SB_WIDTH), BF16),
        compiler_params=_params(3),
        name="sb_attention",
    )(proj, proj, proj)


def _diff_kernel(q_ref, k_ref, v_ref, lq1_ref, lk1_ref, lq2_ref, lk2_ref, g_ref, o_ref,
                 s0_ref, s1_ref, m_ref, acc_ref, *, lam_init):
    head = pl.program_id(1)
    qi = pl.program_id(2)
    blk = DIFF_BLOCK
    tk = DIFF_CHUNK
    q2 = _stack_heads(q_ref[0])
    row = lax.broadcasted_iota(jnp.int32, (2 * blk, tk), 0) % blk
    col = lax.broadcasted_iota(jnp.int32, (2 * blk, tk), 1)

    past_scores = lambda j: _qk(q2, k_ref[0, pl.ds(j * tk, tk), :])
    _sweep_prefetch(past_scores, (s0_ref, s1_ref))

    _softmax_init(m_ref, acc_ref)
    for c in range(blk // tk):
        keys = pl.ds(qi * blk + c * tk, tk)
        s = jnp.where(col + c * tk <= row, _qk(q2, k_ref[0, keys, :]), -jnp.inf)
        _softmax_step(s, v_ref[0, keys, :], m_ref, acc_ref)

    _softmax_sweep(past_scores, lambda j: v_ref[0, pl.ds(j * tk, tk), :],
                   qi * (blk // tk), (s0_ref, s1_ref), m_ref, acc_ref)
    normed = _softmax_result(acc_ref)

    def lam_term(a_ref, b_ref):
        prod = a_ref[pl.ds(head, 1), :] * b_ref[pl.ds(head, 1), :]
        return jnp.exp(jnp.sum(prod, axis=-1, keepdims=True))

    lam = lam_term(lq1_ref, lk1_ref) - lam_term(lq2_ref, lk2_ref) + lam_init
    o = normed[:blk] - lam * normed[blk:]
    o_ref[0] = (_rms(o, g_ref[...]) * (1.0 - lam_init)).astype(o_ref.dtype)


def _diff_attention(proj, n_batch, seq, lq1, lk1, lq2, lk2, subln, lam_init):
    q0 = 3 * SB_WIDTH // LANES
    k0 = q0 + DIFF_WIDTH // LANES
    v0 = k0 + DIFF_WIDTH // LANES
    lam_spec = _resident((DIFF_HEADS, HEAD_DIM), lambda b, h, i: (0, 0))
    return pl.pallas_call(
        functools.partial(_diff_kernel, lam_init=lam_init),
        grid=(n_batch, DIFF_HEADS, seq // DIFF_BLOCK),
        in_specs=[pl.BlockSpec((1, DIFF_BLOCK, LANES), lambda b, h, i: (b, i, q0 + h)),
                  pl.BlockSpec((1, seq, LANES), lambda b, h, i: (b, 0, k0 + h)),
                  pl.BlockSpec((1, seq, LANES), lambda b, h, i: (b, 0, v0 + h)),
                  lam_spec, lam_spec, lam_spec, lam_spec,
                  _resident((1, LANES), lambda b, h, i: (0, 0))],
        out_specs=pl.BlockSpec((1, DIFF_BLOCK, LANES), lambda b, h, i: (b, i, h)),
        out_shape=jax.ShapeDtypeStruct((n_batch, seq, DIFF_WIDTH), BF16),
        scratch_shapes=[pltpu.VMEM((2 * DIFF_BLOCK, DIFF_CHUNK), F32),
                        pltpu.VMEM((2 * DIFF_BLOCK, DIFF_CHUNK), F32),
                        pltpu.VMEM((2 * DIFF_BLOCK, LANES), F32),
                        pltpu.VMEM((2 * DIFF_BLOCK, 2 * LANES), F32)],
        compiler_params=_params(3),
        name="diff_attention",
    )(proj, proj, proj, lq1, lk1, lq2, lk2, subln.reshape(1, LANES))


def _moba_kernel(q_ref, k_ref, v_ref, o_ref, kmean_ref, kext_ref, s0_ref, s1_ref, m_ref, acc_ref, *,
                 n_blocks):
    qi = pl.program_id(2)
    tile = MOBA_TILE
    seq = k_ref.shape[1]

    @pl.when(qi == 0)
    def _():
        kmean_ref[...] = jnp.zeros_like(kmean_ref)

        def fill_mean(b, carry):
            kb = k_ref[0, pl.ds(b * MOBA_BLOCK, MOBA_BLOCK), :].astype(F32)
            kmean_ref[pl.ds(b, 1), :] = jnp.sum(kb, axis=0, keepdims=True) * (1.0 / MOBA_BLOCK)
            return carry

        lax.fori_loop(0, n_blocks, fill_mean, 0)

        def fill_ext(c, carry):
            rows = pl.ds(c * tile, tile)
            key_blk = (c * tile + lax.broadcasted_iota(jnp.int32, (tile, LANES), 0)) // MOBA_BLOCK
            lane = lax.broadcasted_iota(jnp.int32, (tile, LANES), 1)
            kext_ref[rows, :LANES] = k_ref[0, rows, :]
            kext_ref[rows, LANES:] = jnp.where(lane == key_blk, 1.0, 0.0).astype(BF16)
            return carry

        lax.fori_loop(0, seq // tile, fill_ext, 0)

    q2 = _stack_heads(q_ref[0])

    km = kmean_ref[...]
    km_hi = km.astype(BF16)
    km_lo = (km - km_hi.astype(F32)).astype(BF16)
    gate = _qk(km_hi, q2) + _qk(km_lo, q2)
    q_pos = lax.broadcasted_iota(jnp.int32, (LANES, 2 * tile), 1) % tile + qi * tile
    own_blk = q_pos // MOBA_BLOCK
    blk_id = lax.broadcasted_iota(jnp.int32, (LANES, 2 * tile), 0)
    blk_f = blk_id.astype(F32)
    g = jnp.where(blk_id < own_blk, gate, -jnp.inf)
    chosen = blk_id == own_blk
    for _ in range(MOBA_TOPK):
        top = jnp.max(g, axis=0, keepdims=True)
        first = jnp.min(jnp.where(g == top, blk_f, float(2 * LANES)), axis=0, keepdims=True)
        pick = (blk_f == first) & (top > -jnp.inf)
        chosen = chosen | pick
        g = jnp.where(pick, -jnp.inf, g)
    bias = jnp.where(chosen, 0.0, MASK_BIAS).T.astype(BF16)
    q_ext = jnp.concatenate([q2, bias], axis=1)

    past_scores = lambda j: _qk(q_ext, kext_ref[pl.ds(j * tile, tile), :])
    _sweep_prefetch(past_scores, (s0_ref, s1_ref))

    row = lax.broadcasted_iota(jnp.int32, (2 * tile, tile), 0) % tile
    col = lax.broadcasted_iota(jnp.int32, (2 * tile, tile), 1)
    s = jnp.where(col <= row, _qk(q_ext, kext_ref[pl.ds(qi * tile, tile), :]), -jnp.inf)
    _softmax_init(m_ref, acc_ref)
    _softmax_step(s, v_ref[0, pl.ds(qi * tile, tile), :], m_ref, acc_ref)

    _softmax_sweep(past_scores, lambda j: v_ref[0, pl.ds(j * tile, tile), :],
                   qi, (s0_ref, s1_ref), m_ref, acc_ref)
    o = _softmax_result(acc_ref)
    o_ref[0] = jnp.where(_head_lanes((tile, LANES), 0), o[:tile], o[tile:]).astype(o_ref.dtype)


def _moba_attention(proj, n_batch, seq):
    groups = MOBA_WIDTH // LANES
    n_blocks = seq // MOBA_BLOCK
    assert seq % MOBA_TILE == 0 and n_blocks <= LANES, "the block choice rides on 128 bias lanes"
    return pl.pallas_call(
        functools.partial(_moba_kernel, n_blocks=n_blocks),
        grid=(n_batch, groups, seq // MOBA_TILE),
        in_specs=[pl.BlockSpec((1, MOBA_TILE, LANES), lambda b, g, i: (b, i, g)),
                  pl.BlockSpec((1, seq, LANES), lambda b, g, i: (b, 0, groups + g)),
                  pl.BlockSpec((1, seq, LANES), lambda b, g, i: (b, 0, 2 * groups + g))],
        out_specs=pl.BlockSpec((1, MOBA_TILE, LANES), lambda b, g, i: (b, i, g)),
        out_shape=jax.ShapeDtypeStruct((n_batch, seq, MOBA_WIDTH), BF16),
        scratch_shapes=[pltpu.VMEM((LANES, LANES), F32),
                        pltpu.VMEM((seq, 2 * LANES), BF16),
                        pltpu.VMEM((2 * MOBA_TILE, MOBA_TILE), F32),
                        pltpu.VMEM((2 * MOBA_TILE, MOBA_TILE), F32),
                        pltpu.VMEM((2 * MOBA_TILE, LANES), F32),
                        pltpu.VMEM((2 * MOBA_TILE, 2 * LANES), F32)],
        compiler_params=_params(3),
        name="moba_attention",
    )(proj, proj, proj)


def _post_attention_kernel(h_ref, a_ref, b_ref, p_ref, wa_ref, wb_ref, g_mlp_ref, w1_ref, w2_ref,
                           g_ple_ref, wg_ref, wp_ref, g_final_ref, o_ref, *, final_norm):
    h = (h_ref[...]
         + jnp.dot(a_ref[...], wa_ref[...], preferred_element_type=F32)
         + jnp.dot(b_ref[...], wb_ref[...], preferred_element_type=F32))
    u = _rms(h, g_mlp_ref[...]).astype(BF16)
    o_ref[...] = h
    for c in range(w1_ref.shape[1] // PROJ_CHUNK):
        cols = slice(c * PROJ_CHUNK, (c + 1) * PROJ_CHUNK)
        a = jnp.maximum(jnp.dot(u, w1_ref[:, cols], preferred_element_type=F32), 0.0)
        o_ref[...] += jnp.dot((a * a).astype(BF16), w2_ref[cols, :], preferred_element_type=F32)
    h = o_ref[...]
    u = _rms(h, g_ple_ref[...]).astype(BF16)
    gate = jax.nn.sigmoid(jnp.dot(u, wg_ref[...], preferred_element_type=F32))
    emb = jnp.dot(p_ref[...].astype(BF16), wp_ref[...], preferred_element_type=F32)
    out = h + gate * emb
    if final_norm:
        out = _rms(out, g_final_ref[...])
    o_ref[...] = out


def _post_attention(h, a, b, a_col, b_col, p, w_out, g_mlp, w1, w2, g_ple, wg, wp, g_final, final_norm):
    t, d = h.shape
    half = w_out.shape[0] // 2
    dp = p.shape[1]
    d_ff = w1.shape[1]
    tm = min(TOKEN_TILE, t)
    row = lambda i: (i, 0)
    fixed = lambda i: (0, 0)
    return pl.pallas_call(
        functools.partial(_post_attention_kernel, final_norm=final_norm),
        grid=(t // tm,),
        in_specs=[pl.BlockSpec((tm, d), row),
                  pl.BlockSpec((tm, half), lambda i: (i, a_col)),
                  pl.BlockSpec((tm, half), lambda i: (i, b_col)),
                  pl.BlockSpec((tm, dp), row),
                  _resident((half, d), fixed),
                  _resident((half, d), lambda i: (1, 0)),
                  _resident((1, d), fixed),
                  _resident((d, d_ff), fixed),
                  _resident((d_ff, d), fixed),
                  _resident((1, d), fixed),
                  _resident((d, d), fixed),
                  _resident((dp, d), fixed),
                  _resident((1, d), fixed)],
        out_specs=pl.BlockSpec((tm, d), row),
        out_shape=jax.ShapeDtypeStruct((t, d), F32),
        compiler_params=_params(1),
        name="post_attention",
    )(h, a, b, p, w_out, w_out, g_mlp.reshape(1, d), w1, w2, g_ple.reshape(1, d), wg, wp,
      g_final.reshape(1, d))


def kernel(x, p, positions, attn_norm, ab_w_in, ab_w_out, diff_lam_q1, diff_lam_k1, diff_lam_q2,
           diff_lam_k2, diff_subln, moba_w_in, moba_w_out, mlp_norm, w_ff1, w_ff2, ple_norm,
           ple_gate, ple_proj, final_norm):
    n_batch, seq, d = x.shape
    depth = p.shape[0]
    t = n_batch * seq
    assert seq % DIFF_BLOCK == 0 and seq % ATTN_BLOCK == 0 and t % TOKEN_TILE == 0
    cos_t, sin_t = _rope_tables(positions)
    h = x.reshape(t, d)
    even_rope, even_scale = (3, 4), (0, 3)
    odd_rope, odd_scale = (0, 1, 2, 3), (0, 1)
    for i in range(depth):
        j = i // 2
        if i % 2 == 0:
            proj = _norm_proj(h, attn_norm[i], ab_w_in[j].astype(BF16), cos_t, sin_t,
                              even_rope, even_scale).reshape(n_batch, seq, -1)
            lam_init = 0.8 - 0.6 * math.exp(-0.3 * i)
            o_a = _sb_attention(proj, n_batch, seq).reshape(t, SB_WIDTH)
            o_b = _diff_attention(proj, n_batch, seq, diff_lam_q1[j], diff_lam_k1[j], diff_lam_q2[j],
                                  diff_lam_k2[j], diff_subln[j], lam_init).reshape(t, DIFF_WIDTH)
            mixed, w_out = (o_a, o_b, 0, 0), ab_w_out[j]
        else:
            proj = _norm_proj(h, attn_norm[i], moba_w_in[j].astype(BF16), cos_t, sin_t,
                              odd_rope, odd_scale).reshape(n_batch, seq, -1)
            o = _moba_attention(proj, n_batch, seq).reshape(t, MOBA_WIDTH)
            mixed, w_out = (o, o, 0, 1), moba_w_out[j]
        h = _post_attention(h, *mixed, p[i].reshape(t, -1), w_out.astype(BF16), mlp_norm[i],
                            w_ff1[i].astype(BF16), w_ff2[i].astype(BF16), ple_norm[i],
                            ple_gate[i].astype(BF16), ple_proj[i].astype(BF16), final_norm,
                            i == depth - 1)
    return h.reshape(n_batch, seq, d)
```

```python
import functools
import math

import jax
import jax.numpy as jnp
from jax import lax
from jax.experimental import pallas as pl
from jax.experimental.pallas import tpu as pltpu

F32 = jnp.float32
BF16 = jnp.bfloat16

NORM_EPS = 1e-6
ROPE_THETA = 500000.0
HEAD_DIM = 64
ROPE_DIMS = HEAD_DIM // 4
ROPE_HALF = ROPE_DIMS // 2
LANES = 128
QK_SCALE = HEAD_DIM ** -0.5
SB_WIDTH = 512
DIFF_WIDTH = 512
DIFF_HEADS = 4
MOBA_WIDTH = 1024
MOBA_BLOCK = 256
MOBA_TOPK = 3
MOBA_TILE = 512
ATTN_BLOCK = 256
DIFF_BLOCK = 512
DIFF_CHUNK = 512
SWEEP_UNROLL = 4
PROJ_CHUNK = 512
TOKEN_TILE = 512
MASK_BIAS = -1e30
SB_CUTOFF = 110.0
VMEM_LIMIT = 48 * 1024 * 1024


def _params(n_axes):
    return pltpu.CompilerParams(dimension_semantics=("arbitrary",) * n_axes,
                                vmem_limit_bytes=VMEM_LIMIT)


def _resident(shape, index_map):
    return pl.BlockSpec(shape, index_map, pipeline_mode=pl.Buffered(1))


def _rope_table_kernel(pos_ref, invf_ref, cos_ref, sin_ref):
    ang = pos_ref[...].astype(F32) * invf_ref[...]
    lane = lax.broadcasted_iota(jnp.int32, ang.shape, 1) % HEAD_DIM
    c = jnp.cos(ang)
    s = jnp.sin(ang)
    cos_ref[...] = jnp.where(lane < ROPE_DIMS, c, 1.0)
    sin_ref[...] = jnp.where(lane < ROPE_HALF, -s, jnp.where(lane < ROPE_DIMS, s, 0.0))


def _rope_tables(positions):
    t = positions.size
    pos = positions.reshape(t, 1)
    inv_freq = ROPE_THETA ** (-(jnp.arange(0, ROPE_DIMS, 2, dtype=F32) / ROPE_DIMS))
    invf = jnp.tile(inv_freq, LANES // ROPE_HALF).reshape(1, LANES)
    tm = min(2048, t)
    return pl.pallas_call(
        _rope_table_kernel,
        grid=(t // tm,),
        in_specs=[pl.BlockSpec((tm, 1), lambda i: (i, 0)),
                  pl.BlockSpec((1, LANES), lambda i: (0, 0))],
        out_specs=[pl.BlockSpec((tm, LANES), lambda i: (i, 0))] * 2,
        out_shape=[jax.ShapeDtypeStruct((t, LANES), F32)] * 2,
        compiler_params=_params(1),
        name="rope_tables",
    )(pos, invf)


def _rope(a, cos, sin):
    lane = lax.broadcasted_iota(jnp.int32, a.shape, 1) % HEAD_DIM
    partner = jnp.where(lane < ROPE_HALF,
                        pltpu.roll(a, LANES - ROPE_HALF, 1),
                        pltpu.roll(a, ROPE_HALF, 1))
    return a * cos + partner * sin


def _rms(x, g):
    ms = jnp.mean(x * x, axis=-1, keepdims=True)
    return x * lax.rsqrt(ms + NORM_EPS) * g


def _norm_proj_kernel(x_ref, g_ref, w_ref, cos_ref, sin_ref, o_ref, *, rope_chunks, scale_chunks):
    u = _rms(x_ref[...], g_ref[...]).astype(BF16)
    n = w_ref.shape[1]
    for c in range(n // PROJ_CHUNK):
        cols = slice(c * PROJ_CHUNK, (c + 1) * PROJ_CHUNK)
        acc = jnp.dot(u, w_ref[:, cols], preferred_element_type=F32)
        if c in scale_chunks:
            acc = acc * QK_SCALE
        if c in rope_chunks:
            cos = cos_ref[...]
            sin = sin_ref[...]
            for gidx in range(PROJ_CHUNK // LANES):
                lanes = slice(gidx * LANES, (gidx + 1) * LANES)
                o_ref[:, c * PROJ_CHUNK + gidx * LANES:c * PROJ_CHUNK + (gidx + 1) * LANES] = (
                    _rope(acc[:, lanes], cos, sin).astype(BF16))
        else:
            o_ref[:, cols] = acc.astype(BF16)


def _norm_proj(h, g, w, cos_t, sin_t, rope_chunks, scale_chunks):
    t, d = h.shape
    n = w.shape[1]
    tm = min(TOKEN_TILE, t)
    kern = functools.partial(_norm_proj_kernel, rope_chunks=rope_chunks, scale_chunks=scale_chunks)
    return pl.pallas_call(
        kern,
        grid=(t // tm,),
        in_specs=[pl.BlockSpec((tm, d), lambda i: (i, 0)),
                  _resident((1, d), lambda i: (0, 0)),
                  _resident((d, n), lambda i: (0, 0)),
                  pl.BlockSpec((tm, LANES), lambda i: (i, 0)),
                  pl.BlockSpec((tm, LANES), lambda i: (i, 0))],
        out_specs=pl.BlockSpec((tm, n), lambda i: (i, 0)),
        out_shape=jax.ShapeDtypeStruct((t, n), BF16),
        compiler_params=_params(1),
        name="norm_proj",
    )(h, g.reshape(1, d), w, cos_t, sin_t)


def _head_lanes(shape, head):
    lane = lax.broadcasted_iota(jnp.int32, shape, 1)
    return (lane >= head * HEAD_DIM) & (lane < (head + 1) * HEAD_DIM)


def _stack_heads(q):
    return jnp.concatenate([jnp.where(_head_lanes(q.shape, 0), q, jnp.zeros_like(q)),
                            jnp.where(_head_lanes(q.shape, 1), q, jnp.zeros_like(q))], axis=0)


def _qk(q, k):
    return lax.dot_general(q, k, (((1,), (1,)), ((), ())), preferred_element_type=F32)


def _softmax_init(m_ref, acc_ref):
    m_ref[...] = jnp.full(m_ref.shape, -jnp.inf, F32)
    acc_ref[...] = jnp.zeros(acc_ref.shape, F32)


def _softmax_step(s, v, m_ref, acc_ref):
    m = m_ref[...]
    m_new = jnp.maximum(m, jnp.max(s, axis=-1, keepdims=True))
    alpha = jnp.exp(m - m_new)
    alpha = jnp.concatenate([alpha, alpha], axis=1)
    p = jnp.concatenate([jnp.exp(s[:, c * LANES:(c + 1) * LANES] - m_new)
                         for c in range(s.shape[1] // LANES)], axis=1).astype(BF16)
    v_ones = jnp.concatenate([v, jnp.ones_like(v)], axis=1)
    half = p.shape[0] // 2
    acc_ref[:half] = alpha[:half] * acc_ref[:half] + jnp.dot(p[:half], v_ones, preferred_element_type=F32)
    acc_ref[half:] = alpha[half:] * acc_ref[half:] + jnp.dot(p[half:], v_ones, preferred_element_type=F32)
    m_ref[...] = m_new


def _softmax_sweep(score_fn, v_fn, n_chunks, s_refs, m_ref, acc_ref):
    unroll = SWEEP_UNROLL

    def run(first, count, then_prefetch):
        for u in range(count):
            if u + 1 < count or then_prefetch:
                s_refs[(u + 1) % 2][...] = score_fn(first + u + 1)
            _softmax_step(s_refs[u % 2][...], v_fn(first + u), m_ref, acc_ref)

    def body(i, carry):
        run(unroll * i, unroll, True)
        return carry

    n_iters = jnp.maximum((n_chunks - 1) // unroll, 0)
    lax.fori_loop(0, n_iters, body, 0)
    last = unroll * n_iters

    for remaining in range(1, unroll + 1):
        @pl.when((n_chunks > 0) & (n_chunks - last == remaining))
        def _(remaining=remaining):
            run(last, remaining, False)


def _sweep_prefetch(score_fn, s_refs):
    s_refs[0][...] = score_fn(0)


def _softmax_result(acc_ref):
    return acc_ref[:, :LANES] / acc_ref[:, LANES:]


def _sb_kernel(q_ref, k_ref, v_ref, o_ref):
    qi = pl.program_id(2)
    blk = ATTN_BLOCK
    q2 = _stack_heads(q_ref[0])
    row = lax.broadcasted_iota(jnp.int32, (2 * blk, blk), 0) % blk
    col = lax.broadcasted_iota(jnp.int32, (2 * blk, blk), 1)
    past = col < row
    later = (lax.broadcasted_iota(jnp.int32, (blk, blk), 0)
             > lax.broadcasted_iota(jnp.int32, (blk, blk), 1)).astype(BF16)

    def scores(j):
        z = _qk(q2, k_ref[0, pl.ds(j * blk, blk), :])
        return z, -(jnp.maximum(z, 0.0) + jnp.log(1.0 + jnp.exp(-jnp.abs(z))))

    def tail_in_block(log_keep):
        hi = log_keep.astype(BF16)
        lo = (log_keep - hi.astype(F32)).astype(BF16)
        return (jnp.dot(hi, later, preferred_element_type=F32)
                + jnp.dot(lo, later, preferred_element_type=F32))

    def weighted_values(j, w):
        return jnp.dot(w.astype(BF16), v_ref[0, pl.ds(j * blk, blk), :], preferred_element_type=F32)

    prev = jnp.maximum(qi - 1, 0)
    z_d, log_keep_d_all = scores(qi)
    z_p, log_keep_p = scores(prev)
    log_keep_d = jnp.where(past, log_keep_d_all, 0.0)
    c_d = jnp.sum(log_keep_d, axis=-1, keepdims=True)
    w_d = jnp.where(past, jnp.exp(log_keep_d_all + z_d + tail_in_block(log_keep_d)), 0.0)
    w_p = jnp.exp(log_keep_p + z_p + (tail_in_block(log_keep_p) + c_d))
    w_p = jnp.where(qi > 0, w_p, 0.0)
    acc = weighted_values(qi, w_d) + weighted_values(prev, w_p)
    c = c_d + jnp.sum(log_keep_p, axis=-1, keepdims=True)

    def more(carry):
        j, c, _ = carry
        return (j >= 0) & (jnp.max(c) > -SB_CUTOFF)

    def body(carry):
        j, c, acc = carry
        z, log_keep = scores(j)
        w = jnp.exp(log_keep + z + (tail_in_block(log_keep) + c))
        return j - 1, c + jnp.sum(log_keep, axis=-1, keepdims=True), acc + weighted_values(j, w)

    _, c, acc = lax.while_loop(more, body, (qi - 2, c, acc))
    o_ref[0] = jnp.where(_head_lanes((blk, LANES), 0), acc[:blk], acc[blk:]).astype(o_ref.dtype)


def _sb_attention(proj, n_batch, seq):
    groups = SB_WIDTH // LANES
    return pl.pallas_call(
        _sb_kernel,
        grid=(n_batch, groups, seq // ATTN_BLOCK),
        in_specs=[pl.BlockSpec((1, ATTN_BLOCK, LANES), lambda b, g, i: (b, i, g)),
                  pl.BlockSpec((1, seq, LANES), lambda b, g, i: (b, 0, groups + g)),
                  pl.BlockSpec((1, seq, LANES), lambda b, g, i: (b, 0, 2 * groups + g))],
        out_specs=pl.BlockSpec((1, ATTN_BLOCK, LANES), lambda b, g, i: (b, i, g)),
        out_shape=jax.ShapeDtypeStruct((n_batch, seq, SB_WIDTH), BF16),
        compiler_params=_params(3),
        name="sb_attention",
    )(proj, proj, proj)


def _diff_kernel(q_ref, k_ref, v_ref, lq1_ref, lk1_ref, lq2_ref, lk2_ref, g_ref, o_ref,
                 s0_ref, s1_ref, m_ref, acc_ref, *, lam_init):
    head = pl.program_id(1)
    qi = pl.program_id(2)
    blk = DIFF_BLOCK
    tk = DIFF_CHUNK
    q2 = _stack_heads(q_ref[0])
    row = lax.broadcasted_iota(jnp.int32, (2 * blk, tk), 0) % blk
    col = lax.broadcasted_iota(jnp.int32, (2 * blk, tk), 1)

    past_scores = lambda j: _qk(q2, k_ref[0, pl.ds(j * tk, tk), :])

    _softmax_init(m_ref, acc_ref)
    for c in range(blk // tk):
        keys = pl.ds(qi * blk + c * tk, tk)
        s = jnp.where(col + c * tk <= row, _qk(q2, k_ref[0, keys, :]), -jnp.inf)
        if c == blk // tk - 1:
            _sweep_prefetch(past_scores, (s0_ref, s1_ref))
        _softmax_step(s, v_ref[0, keys, :], m_ref, acc_ref)

    _softmax_sweep(past_scores, lambda j: v_ref[0, pl.ds(j * tk, tk), :],
                   qi * (blk // tk), (s0_ref, s1_ref), m_ref, acc_ref)
    normed = _softmax_result(acc_ref)

    def lam_term(a_ref, b_ref):
        prod = a_ref[pl.ds(head, 1), :] * b_ref[pl.ds(head, 1), :]
        return jnp.exp(jnp.sum(prod, axis=-1, keepdims=True))

    lam = lam_term(lq1_ref, lk1_ref) - lam_term(lq2_ref, lk2_ref) + lam_init
    o = normed[:blk] - lam * normed[blk:]
    o_ref[0] = (_rms(o, g_ref[...]) * (1.0 - lam_init)).astype(o_ref.dtype)


def _diff_attention(proj, n_batch, seq, lq1, lk1, lq2, lk2, subln, lam_init):
    q0 = 3 * SB_WIDTH // LANES
    k0 = q0 + DIFF_WIDTH // LANES
    v0 = k0 + DIFF_WIDTH // LANES
    lam_spec = _resident((DIFF_HEADS, HEAD_DIM), lambda b, h, i: (0, 0))
    return pl.pallas_call(
        functools.partial(_diff_kernel, lam_init=lam_init),
        grid=(n_batch, DIFF_HEADS, seq // DIFF_BLOCK),
        in_specs=[pl.BlockSpec((1, DIFF_BLOCK, LANES), lambda b, h, i: (b, i, q0 + h)),
                  pl.BlockSpec((1, seq, LANES), lambda b, h, i: (b, 0, k0 + h)),
                  pl.BlockSpec((1, seq, LANES), lambda b, h, i: (b, 0, v0 + h)),
                  lam_spec, lam_spec, lam_spec, lam_spec,
                  _resident((1, LANES), lambda b, h, i: (0, 0))],
        out_specs=pl.BlockSpec((1, DIFF_BLOCK, LANES), lambda b, h, i: (b, i, h)),
        out_shape=jax.ShapeDtypeStruct((n_batch, seq, DIFF_WIDTH), BF16),
        scratch_shapes=[pltpu.VMEM((2 * DIFF_BLOCK, DIFF_CHUNK), F32),
                        pltpu.VMEM((2 * DIFF_BLOCK, DIFF_CHUNK), F32),
                        pltpu.VMEM((2 * DIFF_BLOCK, LANES), F32),
                        pltpu.VMEM((2 * DIFF_BLOCK, 2 * LANES), F32)],
        compiler_params=_params(3),
        name="diff_attention",
    )(proj, proj, proj, lq1, lk1, lq2, lk2, subln.reshape(1, LANES))


def _moba_kernel(q_ref, k_ref, v_ref, o_ref, kmean_ref, kext_ref, s0_ref, s1_ref, m_ref, acc_ref, *,
                 n_blocks):
    qi = pl.program_id(2)
    tile = MOBA_TILE
    seq = k_ref.shape[1]

    @pl.when(qi == 0)
    def _():
        kmean_ref[...] = jnp.zeros_like(kmean_ref)

        def fill_mean(b, carry):
            kb = k_ref[0, pl.ds(b * MOBA_BLOCK, MOBA_BLOCK), :].astype(F32)
            kmean_ref[pl.ds(b, 1), :] = jnp.sum(kb, axis=0, keepdims=True) * (1.0 / MOBA_BLOCK)
            return carry

        lax.fori_loop(0, n_blocks, fill_mean, 0)

        def fill_ext(c, carry):
            rows = pl.ds(c * tile, tile)
            key_blk = (c * tile + lax.broadcasted_iota(jnp.int32, (tile, LANES), 0)) // MOBA_BLOCK
            lane = lax.broadcasted_iota(jnp.int32, (tile, LANES), 1)
            kext_ref[rows, :LANES] = k_ref[0, rows, :]
            kext_ref[rows, LANES:] = jnp.where(lane == key_blk, 1.0, 0.0).astype(BF16)
            return carry

        lax.fori_loop(0, seq // tile, fill_ext, 0)

    q2 = _stack_heads(q_ref[0])

    km = kmean_ref[...]
    km_hi = km.astype(BF16)
    km_lo = (km - km_hi.astype(F32)).astype(BF16)
    gate = _qk(km_hi, q2) + _qk(km_lo, q2)
    rows = -(-n_blocks // 8) * 8
    q_pos = lax.broadcasted_iota(jnp.int32, (rows, 2 * tile), 1) % tile + qi * tile
    own_blk = q_pos // MOBA_BLOCK
    blk_id = lax.broadcasted_iota(jnp.int32, (rows, 2 * tile), 0)
    blk_f = blk_id.astype(F32)
    g = jnp.where(blk_id < own_blk, gate[:rows], -jnp.inf)
    chosen = blk_id == own_blk
    for _ in range(MOBA_TOPK):
        top = jnp.max(g, axis=0, keepdims=True)
        first = jnp.min(jnp.where(g == top, blk_f, float(2 * LANES)), axis=0, keepdims=True)
        pick = (blk_f == first) & (top > -jnp.inf)
        chosen = chosen | pick
        g = jnp.where(pick, -jnp.inf, g)
    bias = jnp.where(chosen, 0.0, MASK_BIAS)
    if rows < LANES:
        bias = jnp.concatenate([bias, jnp.zeros((LANES - rows, 2 * tile), F32)], axis=0)
    bias = bias.T.astype(BF16)
    q_ext = jnp.concatenate([q2, bias], axis=1)

    past_scores = lambda j: _qk(q_ext, kext_ref[pl.ds(j * tile, tile), :])

    row = lax.broadcasted_iota(jnp.int32, (2 * tile, tile), 0) % tile
    col = lax.broadcasted_iota(jnp.int32, (2 * tile, tile), 1)
    s = jnp.where(col <= row, _qk(q_ext, kext_ref[pl.ds(qi * tile, tile), :]), -jnp.inf)
    _sweep_prefetch(past_scores, (s0_ref, s1_ref))
    _softmax_init(m_ref, acc_ref)
    _softmax_step(s, v_ref[0, pl.ds(qi * tile, tile), :], m_ref, acc_ref)

    _softmax_sweep(past_scores, lambda j: v_ref[0, pl.ds(j * tile, tile), :],
                   qi, (s0_ref, s1_ref), m_ref, acc_ref)
    o = _softmax_result(acc_ref)
    o_ref[0] = jnp.where(_head_lanes((tile, LANES), 0), o[:tile], o[tile:]).astype(o_ref.dtype)


def _moba_attention(proj, n_batch, seq):
    groups = MOBA_WIDTH // LANES
    n_blocks = seq // MOBA_BLOCK
    assert seq % MOBA_TILE == 0 and n_blocks <= LANES, "the block choice rides on 128 bias lanes"
    return pl.pallas_call(
        functools.partial(_moba_kernel, n_blocks=n_blocks),
        grid=(n_batch, groups, seq // MOBA_TILE),
        in_specs=[pl.BlockSpec((1, MOBA_TILE, LANES), lambda b, g, i: (b, i, g)),
                  pl.BlockSpec((1, seq, LANES), lambda b, g, i: (b, 0, groups + g)),
                  pl.BlockSpec((1, seq, LANES), lambda b, g, i: (b, 0, 2 * groups + g))],
        out_specs=pl.BlockSpec((1, MOBA_TILE, LANES), lambda b, g, i: (b, i, g)),
        out_shape=jax.ShapeDtypeStruct((n_batch, seq, MOBA_WIDTH), BF16),
        scratch_shapes=[pltpu.VMEM((LANES, LANES), F32),
                        pltpu.VMEM((seq, 2 * LANES), BF16),
                        pltpu.VMEM((2 * MOBA_TILE, MOBA_TILE), F32),
                        pltpu.VMEM((2 * MOBA_TILE, MOBA_TILE), F32),
                        pltpu.VMEM((2 * MOBA_TILE, LANES), F32),
                        pltpu.VMEM((2 * MOBA_TILE, 2 * LANES), F32)],
        compiler_params=_params(3),
        name="moba_attention",
    )(proj, proj, proj)


def _post_attention_kernel(h_ref, a_ref, b_ref, p_ref, wa_ref, wb_ref, g_mlp_ref, w1_ref, w2_ref,
                           g_ple_ref, wg_ref, wp_ref, g_final_ref, o_ref, *, final_norm):
    h = (h_ref[...]
         + jnp.dot(a_ref[...], wa_ref[...], preferred_element_type=F32)
         + jnp.dot(b_ref[...], wb_ref[...], preferred_element_type=F32))
    u = _rms(h, g_mlp_ref[...]).astype(BF16)
    o_ref[...] = h
    for c in range(w1_ref.shape[1] // PROJ_CHUNK):
        cols = slice(c * PROJ_CHUNK, (c + 1) * PROJ_CHUNK)
        a = jnp.maximum(jnp.dot(u, w1_ref[:, cols], preferred_element_type=F32), 0.0)
        o_ref[...] += jnp.dot((a * a).astype(BF16), w2_ref[cols, :], preferred_element_type=F32)
    h = o_ref[...]
    u = _rms(h, g_ple_ref[...]).astype(BF16)
    gate = jax.nn.sigmoid(jnp.dot(u, wg_ref[...], preferred_element_type=F32))
    emb = jnp.dot(p_ref[...].astype(BF16), wp_ref[...], preferred_element_type=F32)
    out = h + gate * emb
    if final_norm:
        out = _rms(out, g_final_ref[...])
    o_ref[...] = out


def _post_attention(h, a, b, a_col, b_col, p, w_out, g_mlp, w1, w2, g_ple, wg, wp, g_final, final_norm):
    t, d = h.shape
    half = w_out.shape[0] // 2
    dp = p.shape[1]
    d_ff = w1.shape[1]
    tm = min(TOKEN_TILE, t)
    row = lambda i: (i, 0)
    fixed = lambda i: (0, 0)
    return pl.pallas_call(
        functools.partial(_post_attention_kernel, final_norm=final_norm),
        grid=(t // tm,),
        in_specs=[pl.BlockSpec((tm, d), row),
                  pl.BlockSpec((tm, half), lambda i: (i, a_col)),
                  pl.BlockSpec((tm, half), lambda i: (i, b_col)),
                  pl.BlockSpec((tm, dp), row),
                  _resident((half, d), fixed),
                  _resident((half, d), lambda i: (1, 0)),
                  _resident((1, d), fixed),
                  _resident((d, d_ff), fixed),
                  _resident((d_ff, d), fixed),
                  _resident((1, d), fixed),
                  _resident((d, d), fixed),
                  _resident((dp, d), fixed),
                  _resident((1, d), fixed)],
        out_specs=pl.BlockSpec((tm, d), row),
        out_shape=jax.ShapeDtypeStruct((t, d), F32),
        compiler_params=_params(1),
        name="post_attention",
    )(h, a, b, p, w_out, w_out, g_mlp.reshape(1, d), w1, w2, g_ple.reshape(1, d), wg, wp,
      g_final.reshape(1, d))


def kernel(x, p, positions, attn_norm, ab_w_in, ab_w_out, diff_lam_q1, diff_lam_k1, diff_lam_q2,
           diff_lam_k2, diff_subln, moba_w_in, moba_w_out, mlp_norm, w_ff1, w_ff2, ple_norm,
           ple_gate, ple_proj, final_norm):
    n_batch, seq, d = x.shape
    depth = p.shape[0]
    t = n_batch * seq
    assert seq % DIFF_BLOCK == 0 and seq % ATTN_BLOCK == 0 and t % TOKEN_TILE == 0
    cos_t, sin_t = _rope_tables(positions)
    h = x.reshape(t, d)
    even_rope, even_scale = (3, 4), (0, 3)
    odd_rope, odd_scale = (0, 1, 2, 3), (0, 1)
    for i in range(depth):
        j = i // 2
        if i % 2 == 0:
            proj = _norm_proj(h, attn_norm[i], ab_w_in[j].astype(BF16), cos_t, sin_t,
                              even_rope, even_scale).reshape(n_batch, seq, -1)
            lam_init = 0.8 - 0.6 * math.exp(-0.3 * i)
            o_a = _sb_attention(proj, n_batch, seq).reshape(t, SB_WIDTH)
            o_b = _diff_attention(proj, n_batch, seq, diff_lam_q1[j], diff_lam_k1[j], diff_lam_q2[j],
                                  diff_lam_k2[j], diff_subln[j], lam_init).reshape(t, DIFF_WIDTH)
            mixed, w_out = (o_a, o_b, 0, 0), ab_w_out[j]
        else:
            proj = _norm_proj(h, attn_norm[i], moba_w_in[j].astype(BF16), cos_t, sin_t,
                              odd_rope, odd_scale).reshape(n_batch, seq, -1)
            o = _moba_attention(proj, n_batch, seq).reshape(t, MOBA_WIDTH)
            mixed, w_out = (o, o, 0, 1), moba_w_out[j]
        h = _post_attention(h, *mixed, p[i].reshape(t, -1), w_out.astype(BF16), mlp_norm[i],
                            w_ff1[i].astype(BF16), w_ff2[i].astype(BF16), ple_norm[i],
                            ple_gate[i].astype(BF16), ple_proj[i].astype(BF16), final_norm,
                            i == depth - 1)
    return h.reshape(n_batch, seq, d)
```

```python
import functools
import math

import jax
import jax.numpy as jnp
from jax import lax
from jax.experimental import pallas as pl
from jax.experimental.pallas import tpu as pltpu

F32 = jnp.float32
BF16 = jnp.bfloat16

NORM_EPS = 1e-6
ROPE_THETA = 500000.0
HEAD_DIM = 64
ROPE_DIMS = HEAD_DIM // 4
ROPE_HALF = ROPE_DIMS // 2
LANES = 128
QK_SCALE = HEAD_DIM ** -0.5
SB_WIDTH = 512
DIFF_WIDTH = 512
DIFF_HEADS = 4
MOBA_WIDTH = 1024
MOBA_BLOCK = 256
MOBA_TOPK = 3
MOBA_TILE = 512
ATTN_BLOCK = 256
SB_TILE = 512
DIFF_BLOCK = 512
DIFF_CHUNK = 512
SWEEP_UNROLL = 4
PROJ_CHUNK = 512
TOKEN_TILE = 512
MASK_BIAS = -1e30
SB_CUTOFF = 110.0
VMEM_LIMIT = 48 * 1024 * 1024


def _params(n_axes):
    return pltpu.CompilerParams(dimension_semantics=("arbitrary",) * n_axes,
                                vmem_limit_bytes=VMEM_LIMIT)


def _resident(shape, index_map):
    return pl.BlockSpec(shape, index_map, pipeline_mode=pl.Buffered(1))


def _rope_table_kernel(pos_ref, invf_ref, cos_ref, sin_ref):
    ang = pos_ref[...].astype(F32) * invf_ref[...]
    lane = lax.broadcasted_iota(jnp.int32, ang.shape, 1) % HEAD_DIM
    c = jnp.cos(ang)
    s = jnp.sin(ang)
    cos_ref[...] = jnp.where(lane < ROPE_DIMS, c, 1.0)
    sin_ref[...] = jnp.where(lane < ROPE_HALF, -s, jnp.where(lane < ROPE_DIMS, s, 0.0))


def _rope_tables(positions):
    t = positions.size
    pos = positions.reshape(t, 1)
    inv_freq = ROPE_THETA ** (-(jnp.arange(0, ROPE_DIMS, 2, dtype=F32) / ROPE_DIMS))
    invf = jnp.tile(inv_freq, LANES // ROPE_HALF).reshape(1, LANES)
    tm = min(2048, t)
    return pl.pallas_call(
        _rope_table_kernel,
        grid=(t // tm,),
        in_specs=[pl.BlockSpec((tm, 1), lambda i: (i, 0)),
                  pl.BlockSpec((1, LANES), lambda i: (0, 0))],
        out_specs=[pl.BlockSpec((tm, LANES), lambda i: (i, 0))] * 2,
        out_shape=[jax.ShapeDtypeStruct((t, LANES), F32)] * 2,
        compiler_params=_params(1),
        name="rope_tables",
    )(pos, invf)


def _rope(a, cos, sin):
    lane = lax.broadcasted_iota(jnp.int32, a.shape, 1) % HEAD_DIM
    partner = jnp.where(lane < ROPE_HALF,
                        pltpu.roll(a, LANES - ROPE_HALF, 1),
                        pltpu.roll(a, ROPE_HALF, 1))
    return a * cos + partner * sin


def _rms(x, g):
    ms = jnp.mean(x * x, axis=-1, keepdims=True)
    return x * lax.rsqrt(ms + NORM_EPS) * g


def _norm_proj_kernel(x_ref, g_ref, w_ref, cos_ref, sin_ref, o_ref, *, rope_chunks, scale_chunks):
    u = _rms(x_ref[...], g_ref[...]).astype(BF16)
    n = w_ref.shape[1]
    for c in range(n // PROJ_CHUNK):
        cols = slice(c * PROJ_CHUNK, (c + 1) * PROJ_CHUNK)
        acc = jnp.dot(u, w_ref[:, cols], preferred_element_type=F32)
        if c in scale_chunks:
            acc = acc * QK_SCALE
        if c in rope_chunks:
            cos = cos_ref[...]
            sin = sin_ref[...]
            for gidx in range(PROJ_CHUNK // LANES):
                lanes = slice(gidx * LANES, (gidx + 1) * LANES)
                o_ref[:, c * PROJ_CHUNK + gidx * LANES:c * PROJ_CHUNK + (gidx + 1) * LANES] = (
                    _rope(acc[:, lanes], cos, sin).astype(BF16))
        else:
            o_ref[:, cols] = acc.astype(BF16)


def _norm_proj(h, g, w, cos_t, sin_t, rope_chunks, scale_chunks):
    t, d = h.shape
    n = w.shape[1]
    tm = min(TOKEN_TILE, t)
    kern = functools.partial(_norm_proj_kernel, rope_chunks=rope_chunks, scale_chunks=scale_chunks)
    return pl.pallas_call(
        kern,
        grid=(t // tm,),
        in_specs=[pl.BlockSpec((tm, d), lambda i: (i, 0)),
                  _resident((1, d), lambda i: (0, 0)),
                  _resident((d, n), lambda i: (0, 0)),
                  pl.BlockSpec((tm, LANES), lambda i: (i, 0)),
                  pl.BlockSpec((tm, LANES), lambda i: (i, 0))],
        out_specs=pl.BlockSpec((tm, n), lambda i: (i, 0)),
        out_shape=jax.ShapeDtypeStruct((t, n), BF16),
        compiler_params=_params(1),
        name="norm_proj",
    )(h, g.reshape(1, d), w, cos_t, sin_t)


def _head_lanes(shape, head):
    lane = lax.broadcasted_iota(jnp.int32, shape, 1)
    return (lane >= head * HEAD_DIM) & (lane < (head + 1) * HEAD_DIM)


def _stack_heads(q):
    return jnp.concatenate([jnp.where(_head_lanes(q.shape, 0), q, jnp.zeros_like(q)),
                            jnp.where(_head_lanes(q.shape, 1), q, jnp.zeros_like(q))], axis=0)


def _qk(q, k):
    return lax.dot_general(q, k, (((1,), (1,)), ((), ())), preferred_element_type=F32)


def _softmax_init(m_ref, acc_ref):
    m_ref[...] = jnp.full(m_ref.shape, -jnp.inf, F32)
    acc_ref[...] = jnp.zeros(acc_ref.shape, F32)


def _softmax_step(s, v, m_ref, acc_ref):
    m = m_ref[...]
    m_new = jnp.maximum(m, jnp.max(s, axis=-1, keepdims=True))
    alpha = jnp.exp(m - m_new)
    alpha = jnp.concatenate([alpha, alpha], axis=1)
    p = jnp.concatenate([jnp.exp(s[:, c * LANES:(c + 1) * LANES] - m_new)
                         for c in range(s.shape[1] // LANES)], axis=1).astype(BF16)
    v_ones = jnp.concatenate([v, jnp.ones_like(v)], axis=1)
    half = p.shape[0] // 2
    acc_ref[:half] = alpha[:half] * acc_ref[:half] + jnp.dot(p[:half], v_ones, preferred_element_type=F32)
    acc_ref[half:] = alpha[half:] * acc_ref[half:] + jnp.dot(p[half:], v_ones, preferred_element_type=F32)
    m_ref[...] = m_new


def _softmax_sweep(score_fn, v_fn, n_chunks, s_refs, m_ref, acc_ref):
    unroll = SWEEP_UNROLL

    def run(first, count, then_prefetch):
        for u in range(count):
            if u + 1 < count or then_prefetch:
                s_refs[(u + 1) % 2][...] = score_fn(first + u + 1)
            _softmax_step(s_refs[u % 2][...], v_fn(first + u), m_ref, acc_ref)

    def body(i, carry):
        run(unroll * i, unroll, True)
        return carry

    n_iters = jnp.maximum((n_chunks - 1) // unroll, 0)
    lax.fori_loop(0, n_iters, body, 0)
    last = unroll * n_iters

    for remaining in range(1, unroll + 1):
        @pl.when((n_chunks > 0) & (n_chunks - last == remaining))
        def _(remaining=remaining):
            run(last, remaining, False)


def _sweep_prefetch(score_fn, s_refs):
    s_refs[0][...] = score_fn(0)


def _softmax_result(acc_ref):
    return acc_ref[:, :LANES] / acc_ref[:, LANES:]


def _sb_kernel(q_ref, k_ref, v_ref, o_ref):
    blk = ATTN_BLOCK
    row = lax.broadcasted_iota(jnp.int32, (2 * blk, blk), 0) % blk
    col = lax.broadcasted_iota(jnp.int32, (2 * blk, blk), 1)
    past = col < row
    later = (lax.broadcasted_iota(jnp.int32, (blk, blk), 0)
             > lax.broadcasted_iota(jnp.int32, (blk, blk), 1)).astype(BF16)

    def scores(q2, j):
        z = _qk(q2, k_ref[0, pl.ds(j * blk, blk), :])
        return z, -(jnp.maximum(z, 0.0) + jnp.log(1.0 + jnp.exp(-jnp.abs(z))))

    def tail_in_block(log_keep):
        hi = log_keep.astype(BF16)
        lo = (log_keep - hi.astype(F32)).astype(BF16)
        return (jnp.dot(hi, later, preferred_element_type=F32)
                + jnp.dot(lo, later, preferred_element_type=F32))

    def weighted_values(j, w):
        return jnp.dot(w.astype(BF16), v_ref[0, pl.ds(j * blk, blk), :], preferred_element_type=F32)

    def first_two_blocks(q2, qi):
        prev = jnp.maximum(qi - 1, 0)
        z_d, log_keep_d_all = scores(q2, qi)
        z_p, log_keep_p = scores(q2, prev)
        log_keep_d = jnp.where(past, log_keep_d_all, 0.0)
        c_d = jnp.sum(log_keep_d, axis=-1, keepdims=True)
        w_d = jnp.where(past, jnp.exp(log_keep_d_all + z_d + tail_in_block(log_keep_d)), 0.0)
        w_p = jnp.exp(log_keep_p + z_p + (tail_in_block(log_keep_p) + c_d))
        w_p = jnp.where(qi > 0, w_p, 0.0)
        acc = weighted_values(qi, w_d) + weighted_values(prev, w_p)
        return c_d + jnp.sum(log_keep_p, axis=-1, keepdims=True), acc

    def earlier_blocks(q2, qi, c, acc):
        def more(carry):
            j, c, _ = carry
            return (j >= 0) & (jnp.max(c) > -SB_CUTOFF)

        def body(carry):
            j, c, acc = carry
            z, log_keep = scores(q2, j)
            w = jnp.exp(log_keep + z + (tail_in_block(log_keep) + c))
            return j - 1, c + jnp.sum(log_keep, axis=-1, keepdims=True), acc + weighted_values(j, w)

        return lax.while_loop(more, body, (qi - 2, c, acc))[2]

    blocks = []
    for sub in range(SB_TILE // blk):
        qi = pl.program_id(2) * (SB_TILE // blk) + sub
        q2 = _stack_heads(q_ref[0, sub * blk:(sub + 1) * blk, :])
        blocks.append((q2, qi) + first_two_blocks(q2, qi))
    for sub, (q2, qi, c, acc) in enumerate(blocks):
        acc = earlier_blocks(q2, qi, c, acc)
        o_ref[0, sub * blk:(sub + 1) * blk, :] = jnp.where(
            _head_lanes((blk, LANES), 0), acc[:blk], acc[blk:]).astype(o_ref.dtype)


def _sb_attention(proj, n_batch, seq):
    groups = SB_WIDTH // LANES
    return pl.pallas_call(
        _sb_kernel,
        grid=(n_batch, groups, seq // SB_TILE),
        in_specs=[pl.BlockSpec((1, SB_TILE, LANES), lambda b, g, i: (b, i, g)),
                  pl.BlockSpec((1, seq, LANES), lambda b, g, i: (b, 0, groups + g)),
                  pl.BlockSpec((1, seq, LANES), lambda b, g, i: (b, 0, 2 * groups + g))],
        out_specs=pl.BlockSpec((1, SB_TILE, LANES), lambda b, g, i: (b, i, g)),
        out_shape=jax.ShapeDtypeStruct((n_batch, seq, SB_WIDTH), BF16),
        compiler_params=_params(3),
        name="sb_attention",
    )(proj, proj, proj)


def _diff_kernel(q_ref, k_ref, v_ref, lq1_ref, lk1_ref, lq2_ref, lk2_ref, g_ref, o_ref,
                 s0_ref, s1_ref, m_ref, acc_ref, *, lam_init):
    head = pl.program_id(1)
    qi = pl.program_id(2)
    blk = DIFF_BLOCK
    tk = DIFF_CHUNK
    q2 = _stack_heads(q_ref[0])
    row = lax.broadcasted_iota(jnp.int32, (2 * blk, tk), 0) % blk
    col = lax.broadcasted_iota(jnp.int32, (2 * blk, tk), 1)

    past_scores = lambda j: _qk(q2, k_ref[0, pl.ds(j * tk, tk), :])

    _softmax_init(m_ref, acc_ref)
    for c in range(blk // tk):
        keys = pl.ds(qi * blk + c * tk, tk)
        s = jnp.where(col + c * tk <= row, _qk(q2, k_ref[0, keys, :]), -jnp.inf)
        if c == blk // tk - 1:
            _sweep_prefetch(past_scores, (s0_ref, s1_ref))
        _softmax_step(s, v_ref[0, keys, :], m_ref, acc_ref)

    _softmax_sweep(past_scores, lambda j: v_ref[0, pl.ds(j * tk, tk), :],
                   qi * (blk // tk), (s0_ref, s1_ref), m_ref, acc_ref)
    normed = _softmax_result(acc_ref)

    def lam_term(a_ref, b_ref):
        prod = a_ref[pl.ds(head, 1), :] * b_ref[pl.ds(head, 1), :]
        return jnp.exp(jnp.sum(prod, axis=-1, keepdims=True))

    lam = lam_term(lq1_ref, lk1_ref) - lam_term(lq2_ref, lk2_ref) + lam_init
    o = normed[:blk] - lam * normed[blk:]
    o_ref[0] = (_rms(o, g_ref[...]) * (1.0 - lam_init)).astype(o_ref.dtype)


def _diff_attention(proj, n_batch, seq, lq1, lk1, lq2, lk2, subln, lam_init):
    q0 = 3 * SB_WIDTH // LANES
    k0 = q0 + DIFF_WIDTH // LANES
    v0 = k0 + DIFF_WIDTH // LANES
    lam_spec = _resident((DIFF_HEADS, HEAD_DIM), lambda b, h, i: (0, 0))
    return pl.pallas_call(
        functools.partial(_diff_kernel, lam_init=lam_init),
        grid=(n_batch, DIFF_HEADS, seq // DIFF_BLOCK),
        in_specs=[pl.BlockSpec((1, DIFF_BLOCK, LANES), lambda b, h, i: (b, i, q0 + h)),
                  pl.BlockSpec((1, seq, LANES), lambda b, h, i: (b, 0, k0 + h)),
                  pl.BlockSpec((1, seq, LANES), lambda b, h, i: (b, 0, v0 + h)),
                  lam_spec, lam_spec, lam_spec, lam_spec,
                  _resident((1, LANES), lambda b, h, i: (0, 0))],
        out_specs=pl.BlockSpec((1, DIFF_BLOCK, LANES), lambda b, h, i: (b, i, h)),
        out_shape=jax.ShapeDtypeStruct((n_batch, seq, DIFF_WIDTH), BF16),
        scratch_shapes=[pltpu.VMEM((2 * DIFF_BLOCK, DIFF_CHUNK), F32),
                        pltpu.VMEM((2 * DIFF_BLOCK, DIFF_CHUNK), F32),
                        pltpu.VMEM((2 * DIFF_BLOCK, LANES), F32),
                        pltpu.VMEM((2 * DIFF_BLOCK, 2 * LANES), F32)],
        compiler_params=_params(3),
        name="diff_attention",
    )(proj, proj, proj, lq1, lk1, lq2, lk2, subln.reshape(1, LANES))


def _moba_kernel(q_ref, k_ref, v_ref, o_ref, kmean_ref, kext_ref, s0_ref, s1_ref, m_ref, acc_ref, *,
                 n_blocks):
    qi = pl.program_id(2)
    tile = MOBA_TILE
    seq = k_ref.shape[1]

    @pl.when(qi == 0)
    def _():
        kmean_ref[...] = jnp.zeros_like(kmean_ref)

        def fill_mean(b, carry):
            kb = k_ref[0, pl.ds(b * MOBA_BLOCK, MOBA_BLOCK), :].astype(F32)
            kmean_ref[pl.ds(b, 1), :] = jnp.sum(kb, axis=0, keepdims=True) * (1.0 / MOBA_BLOCK)
            return carry

        lax.fori_loop(0, n_blocks, fill_mean, 0)

        def fill_ext(c, carry):
            rows = pl.ds(c * tile, tile)
            key_blk = (c * tile + lax.broadcasted_iota(jnp.int32, (tile, LANES), 0)) // MOBA_BLOCK
            lane = lax.broadcasted_iota(jnp.int32, (tile, LANES), 1)
            kext_ref[rows, :LANES] = k_ref[0, rows, :]
            kext_ref[rows, LANES:] = jnp.where(lane == key_blk, 1.0, 0.0).astype(BF16)
            return carry

        lax.fori_loop(0, seq // tile, fill_ext, 0)

    q2 = _stack_heads(q_ref[0])

    km = kmean_ref[...]
    km_hi = km.astype(BF16)
    km_lo = (km - km_hi.astype(F32)).astype(BF16)
    gate = _qk(km_hi, q2) + _qk(km_lo, q2)
    rows = -(-n_blocks // 8) * 8
    q_pos = lax.broadcasted_iota(jnp.int32, (rows, 2 * tile), 1) % tile + qi * tile
    own_blk = q_pos // MOBA_BLOCK
    blk_id = lax.broadcasted_iota(jnp.int32, (rows, 2 * tile), 0)
    blk_f = blk_id.astype(F32)
    g = jnp.where(blk_id < own_blk, gate[:rows], -jnp.inf)
    chosen = blk_id == own_blk
    for _ in range(MOBA_TOPK):
        top = jnp.max(g, axis=0, keepdims=True)
        first = jnp.min(jnp.where(g == top, blk_f, float(2 * LANES)), axis=0, keepdims=True)
        pick = (blk_f == first) & (top > -jnp.inf)
        chosen = chosen | pick
        g = jnp.where(pick, -jnp.inf, g)
    bias = jnp.where(chosen, 0.0, MASK_BIAS)
    if rows < LANES:
        bias = jnp.concatenate([bias, jnp.zeros((LANES - rows, 2 * tile), F32)], axis=0)
    bias = bias.T.astype(BF16)
    q_ext = jnp.concatenate([q2, bias], axis=1)

    past_scores = lambda j: _qk(q_ext, kext_ref[pl.ds(j * tile, tile), :])

    row = lax.broadcasted_iota(jnp.int32, (2 * tile, tile), 0) % tile
    col = lax.broadcasted_iota(jnp.int32, (2 * tile, tile), 1)
    s = jnp.where(col <= row, _qk(q_ext, kext_ref[pl.ds(qi * tile, tile), :]), -jnp.inf)
    _sweep_prefetch(past_scores, (s0_ref, s1_ref))
    _softmax_init(m_ref, acc_ref)
    _softmax_step(s, v_ref[0, pl.ds(qi * tile, tile), :], m_ref, acc_ref)

    _softmax_sweep(past_scores, lambda j: v_ref[0, pl.ds(j * tile, tile), :],
                   qi, (s0_ref, s1_ref), m_ref, acc_ref)
    o = _softmax_result(acc_ref)
    o_ref[0] = jnp.where(_head_lanes((tile, LANES), 0), o[:tile], o[tile:]).astype(o_ref.dtype)


def _moba_attention(proj, n_batch, seq):
    groups = MOBA_WIDTH // LANES
    n_blocks = seq // MOBA_BLOCK
    assert seq % MOBA_TILE == 0 and n_blocks <= LANES, "the block choice rides on 128 bias lanes"
    return pl.pallas_call(
        functools.partial(_moba_kernel, n_blocks=n_blocks),
        grid=(n_batch, groups, seq // MOBA_TILE),
        in_specs=[pl.BlockSpec((1, MOBA_TILE, LANES), lambda b, g, i: (b, i, g)),
                  pl.BlockSpec((1, seq, LANES), lambda b, g, i: (b, 0, groups + g)),
                  pl.BlockSpec((1, seq, LANES), lambda b, g, i: (b, 0, 2 * groups + g))],
        out_specs=pl.BlockSpec((1, MOBA_TILE, LANES), lambda b, g, i: (b, i, g)),
        out_shape=jax.ShapeDtypeStruct((n_batch, seq, MOBA_WIDTH), BF16),
        scratch_shapes=[pltpu.VMEM((LANES, LANES), F32),
                        pltpu.VMEM((seq, 2 * LANES), BF16),
                        pltpu.VMEM((2 * MOBA_TILE, MOBA_TILE), F32),
                        pltpu.VMEM((2 * MOBA_TILE, MOBA_TILE), F32),
                        pltpu.VMEM((2 * MOBA_TILE, LANES), F32),
                        pltpu.VMEM((2 * MOBA_TILE, 2 * LANES), F32)],
        compiler_params=_params(3),
        name="moba_attention",
    )(proj, proj, proj)


def _post_attention_kernel(h_ref, a_ref, b_ref, p_ref, wa_ref, wb_ref, g_mlp_ref, w1_ref, w2_ref,
                           g_ple_ref, wg_ref, wp_ref, g_final_ref, o_ref, *, final_norm):
    h = (h_ref[...]
         + jnp.dot(a_ref[...], wa_ref[...], preferred_element_type=F32)
         + jnp.dot(b_ref[...], wb_ref[...], preferred_element_type=F32))
    u = _rms(h, g_mlp_ref[...]).astype(BF16)
    o_ref[...] = h
    for c in range(w1_ref.shape[1] // PROJ_CHUNK):
        cols = slice(c * PROJ_CHUNK, (c + 1) * PROJ_CHUNK)
        a = jnp.maximum(jnp.dot(u, w1_ref[:, cols], preferred_element_type=F32), 0.0)
        o_ref[...] += jnp.dot((a * a).astype(BF16), w2_ref[cols, :], preferred_element_type=F32)
    h = o_ref[...]
    u = _rms(h, g_ple_ref[...]).astype(BF16)
    gate = jax.nn.sigmoid(jnp.dot(u, wg_ref[...], preferred_element_type=F32))
    emb = jnp.dot(p_ref[...].astype(BF16), wp_ref[...], preferred_element_type=F32)
    out = h + gate * emb
    if final_norm:
        out = _rms(out, g_final_ref[...])
    o_ref[...] = out


def _post_attention(h, a, b, a_col, b_col, p, w_out, g_mlp, w1, w2, g_ple, wg, wp, g_final, final_norm):
    t, d = h.shape
    half = w_out.shape[0] // 2
    dp = p.shape[1]
    d_ff = w1.shape[1]
    tm = min(TOKEN_TILE, t)
    row = lambda i: (i, 0)
    fixed = lambda i: (0, 0)
    return pl.pallas_call(
        functools.partial(_post_attention_kernel, final_norm=final_norm),
        grid=(t // tm,),
        in_specs=[pl.BlockSpec((tm, d), row),
                  pl.BlockSpec((tm, half), lambda i: (i, a_col)),
                  pl.BlockSpec((tm, half), lambda i: (i, b_col)),
                  pl.BlockSpec((tm, dp), row),
                  _resident((half, d), fixed),
                  _resident((half, d), lambda i: (1, 0)),
                  _resident((1, d), fixed),
                  _resident((d, d_ff), fixed),
                  _resident((d_ff, d), fixed),
                  _resident((1, d), fixed),
                  _resident((d, d), fixed),
                  _resident((dp, d), fixed),
                  _resident((1, d), fixed)],
        out_specs=pl.BlockSpec((tm, d), row),
        out_shape=jax.ShapeDtypeStruct((t, d), F32),
        compiler_params=_params(1),
        name="post_attention",
    )(h, a, b, p, w_out, w_out, g_mlp.reshape(1, d), w1, w2, g_ple.reshape(1, d), wg, wp,
      g_final.reshape(1, d))


def kernel(x, p, positions, attn_norm, ab_w_in, ab_w_out, diff_lam_q1, diff_lam_k1, diff_lam_q2,
           diff_lam_k2, diff_subln, moba_w_in, moba_w_out, mlp_norm, w_ff1, w_ff2, ple_norm,
           ple_gate, ple_proj, final_norm):
    n_batch, seq, d = x.shape
    depth = p.shape[0]
    t = n_batch * seq
    assert seq % DIFF_BLOCK == 0 and seq % SB_TILE == 0 and t % TOKEN_TILE == 0
    cos_t, sin_t = _rope_tables(positions)
    h = x.reshape(t, d)
    even_rope, even_scale = (3, 4), (0, 3)
    odd_rope, odd_scale = (0, 1, 2, 3), (0, 1)
    for i in range(depth):
        j = i // 2
        if i % 2 == 0:
            proj = _norm_proj(h, attn_norm[i], ab_w_in[j].astype(BF16), cos_t, sin_t,
                              even_rope, even_scale).reshape(n_batch, seq, -1)
            lam_init = 0.8 - 0.6 * math.exp(-0.3 * i)
            o_a = _sb_attention(proj, n_batch, seq).reshape(t, SB_WIDTH)
            o_b = _diff_attention(proj, n_batch, seq, diff_lam_q1[j], diff_lam_k1[j], diff_lam_q2[j],
                                  diff_lam_k2[j], diff_subln[j], lam_init).reshape(t, DIFF_WIDTH)
            mixed, w_out = (o_a, o_b, 0, 0), ab_w_out[j]
        else:
            proj = _norm_proj(h, attn_norm[i], moba_w_in[j].astype(BF16), cos_t, sin_t,
                              odd_rope, odd_scale).reshape(n_batch, seq, -1)
            o = _moba_attention(proj, n_batch, seq).reshape(t, MOBA_WIDTH)
            mixed, w_out = (o, o, 0, 1), moba_w_out[j]
        h = _post_attention(h, *mixed, p[i].reshape(t, -1), w_out.astype(BF16), mlp_norm[i],
                            w_ff1[i].astype(BF16), w_ff2[i].astype(BF16), ple_norm[i],
                            ple_gate[i].astype(BF16), ple_proj[i].astype(BF16), final_norm,
                            i == depth - 1)
    return h.reshape(n_batch, seq, d)
```

```python
import functools
import math

import jax
import jax.numpy as jnp
from jax import lax
from jax.experimental import pallas as pl
from jax.experimental.pallas import tpu as pltpu

F32 = jnp.float32
BF16 = jnp.bfloat16

NORM_EPS = 1e-6
ROPE_THETA = 500000.0
HEAD_DIM = 64
ROPE_DIMS = HEAD_DIM // 4
ROPE_HALF = ROPE_DIMS // 2
LANES = 128
QK_SCALE = HEAD_DIM ** -0.5
SB_WIDTH = 512
DIFF_WIDTH = 512
DIFF_HEADS = 4
MOBA_WIDTH = 1024
MOBA_BLOCK = 256
MOBA_TOPK = 3
MOBA_TILE = 512
ATTN_BLOCK = 256
SB_TILE = 1024
DIFF_BLOCK = 512
DIFF_CHUNK = 512
SWEEP_UNROLL = 4
PROJ_CHUNK = 512
TOKEN_TILE = 512
MASK_BIAS = -1e30
SB_CUTOFF = 110.0
VMEM_LIMIT = 48 * 1024 * 1024


def _params(n_axes):
    return pltpu.CompilerParams(dimension_semantics=("arbitrary",) * n_axes,
                                vmem_limit_bytes=VMEM_LIMIT)


def _resident(shape, index_map):
    return pl.BlockSpec(shape, index_map, pipeline_mode=pl.Buffered(1))


def _rope_table_kernel(pos_ref, invf_ref, cos_ref, sin_ref):
    ang = pos_ref[...].astype(F32) * invf_ref[...]
    lane = lax.broadcasted_iota(jnp.int32, ang.shape, 1) % HEAD_DIM
    c = jnp.cos(ang)
    s = jnp.sin(ang)
    cos_ref[...] = jnp.where(lane < ROPE_DIMS, c, 1.0)
    sin_ref[...] = jnp.where(lane < ROPE_HALF, -s, jnp.where(lane < ROPE_DIMS, s, 0.0))


def _rope_tables(positions):
    t = positions.size
    pos = positions.reshape(t, 1)
    inv_freq = ROPE_THETA ** (-(jnp.arange(0, ROPE_DIMS, 2, dtype=F32) / ROPE_DIMS))
    invf = jnp.tile(inv_freq, LANES // ROPE_HALF).reshape(1, LANES)
    tm = min(2048, t)
    return pl.pallas_call(
        _rope_table_kernel,
        grid=(t // tm,),
        in_specs=[pl.BlockSpec((tm, 1), lambda i: (i, 0)),
                  pl.BlockSpec((1, LANES), lambda i: (0, 0))],
        out_specs=[pl.BlockSpec((tm, LANES), lambda i: (i, 0))] * 2,
        out_shape=[jax.ShapeDtypeStruct((t, LANES), F32)] * 2,
        compiler_params=_params(1),
        name="rope_tables",
    )(pos, invf)


def _rope(a, cos, sin):
    lane = lax.broadcasted_iota(jnp.int32, a.shape, 1) % HEAD_DIM
    partner = jnp.where(lane < ROPE_HALF,
                        pltpu.roll(a, LANES - ROPE_HALF, 1),
                        pltpu.roll(a, ROPE_HALF, 1))
    return a * cos + partner * sin


def _rms(x, g):
    ms = jnp.mean(x * x, axis=-1, keepdims=True)
    return x * lax.rsqrt(ms + NORM_EPS) * g


def _norm_proj_kernel(x_ref, g_ref, w_ref, cos_ref, sin_ref, o_ref, *, rope_chunks, scale_chunks):
    u = _rms(x_ref[...], g_ref[...]).astype(BF16)
    n = w_ref.shape[1]
    for c in range(n // PROJ_CHUNK):
        cols = slice(c * PROJ_CHUNK, (c + 1) * PROJ_CHUNK)
        acc = jnp.dot(u, w_ref[:, cols], preferred_element_type=F32)
        if c in scale_chunks:
            acc = acc * QK_SCALE
        if c in rope_chunks:
            cos = cos_ref[...]
            sin = sin_ref[...]
            for gidx in range(PROJ_CHUNK // LANES):
                lanes = slice(gidx * LANES, (gidx + 1) * LANES)
                o_ref[:, c * PROJ_CHUNK + gidx * LANES:c * PROJ_CHUNK + (gidx + 1) * LANES] = (
                    _rope(acc[:, lanes], cos, sin).astype(BF16))
        else:
            o_ref[:, cols] = acc.astype(BF16)


def _norm_proj(h, g, w, cos_t, sin_t, rope_chunks, scale_chunks):
    t, d = h.shape
    n = w.shape[1]
    tm = min(TOKEN_TILE, t)
    kern = functools.partial(_norm_proj_kernel, rope_chunks=rope_chunks, scale_chunks=scale_chunks)
    return pl.pallas_call(
        kern,
        grid=(t // tm,),
        in_specs=[pl.BlockSpec((tm, d), lambda i: (i, 0)),
                  _resident((1, d), lambda i: (0, 0)),
                  _resident((d, n), lambda i: (0, 0)),
                  pl.BlockSpec((tm, LANES), lambda i: (i, 0)),
                  pl.BlockSpec((tm, LANES), lambda i: (i, 0))],
        out_specs=pl.BlockSpec((tm, n), lambda i: (i, 0)),
        out_shape=jax.ShapeDtypeStruct((t, n), BF16),
        compiler_params=_params(1),
        name="norm_proj",
    )(h, g.reshape(1, d), w, cos_t, sin_t)


def _head_lanes(shape, head):
    lane = lax.broadcasted_iota(jnp.int32, shape, 1)
    return (lane >= head * HEAD_DIM) & (lane < (head + 1) * HEAD_DIM)


def _stack_heads(q):
    return jnp.concatenate([jnp.where(_head_lanes(q.shape, 0), q, jnp.zeros_like(q)),
                            jnp.where(_head_lanes(q.shape, 1), q, jnp.zeros_like(q))], axis=0)


def _qk(q, k):
    return lax.dot_general(q, k, (((1,), (1,)), ((), ())), preferred_element_type=F32)


def _softmax_init(m_ref, acc_ref):
    m_ref[...] = jnp.full(m_ref.shape, -jnp.inf, F32)
    acc_ref[...] = jnp.zeros(acc_ref.shape, F32)


def _softmax_step(s, v, m_ref, acc_ref):
    m = m_ref[...]
    m_new = jnp.maximum(m, jnp.max(s, axis=-1, keepdims=True))
    alpha = jnp.exp(m - m_new)
    alpha = jnp.concatenate([alpha, alpha], axis=1)
    p = jnp.concatenate([jnp.exp(s[:, c * LANES:(c + 1) * LANES] - m_new)
                         for c in range(s.shape[1] // LANES)], axis=1).astype(BF16)
    v_ones = jnp.concatenate([v, jnp.ones_like(v)], axis=1)
    half = p.shape[0] // 2
    acc_ref[:half] = alpha[:half] * acc_ref[:half] + jnp.dot(p[:half], v_ones, preferred_element_type=F32)
    acc_ref[half:] = alpha[half:] * acc_ref[half:] + jnp.dot(p[half:], v_ones, preferred_element_type=F32)
    m_ref[...] = m_new


def _softmax_sweep(score_fn, v_fn, n_chunks, s_refs, m_ref, acc_ref):
    unroll = SWEEP_UNROLL

    def run(first, count, then_prefetch):
        for u in range(count):
            if u + 1 < count or then_prefetch:
                s_refs[(u + 1) % 2][...] = score_fn(first + u + 1)
            _softmax_step(s_refs[u % 2][...], v_fn(first + u), m_ref, acc_ref)

    def body(i, carry):
        run(unroll * i, unroll, True)
        return carry

    n_iters = jnp.maximum((n_chunks - 1) // unroll, 0)
    lax.fori_loop(0, n_iters, body, 0)
    last = unroll * n_iters

    for remaining in range(1, unroll + 1):
        @pl.when((n_chunks > 0) & (n_chunks - last == remaining))
        def _(remaining=remaining):
            run(last, remaining, False)


def _sweep_prefetch(score_fn, s_refs):
    s_refs[0][...] = score_fn(0)


def _softmax_result(acc_ref):
    return acc_ref[:, :LANES] / acc_ref[:, LANES:]


def _sb_kernel(q_ref, k_ref, v_ref, o_ref):
    blk = ATTN_BLOCK
    row = lax.broadcasted_iota(jnp.int32, (2 * blk, blk), 0) % blk
    col = lax.broadcasted_iota(jnp.int32, (2 * blk, blk), 1)
    past = col < row
    later = (lax.broadcasted_iota(jnp.int32, (blk, blk), 0)
             > lax.broadcasted_iota(jnp.int32, (blk, blk), 1)).astype(BF16)

    def scores(q2, j):
        z = _qk(q2, k_ref[0, pl.ds(j * blk, blk), :])
        return z, -(jnp.maximum(z, 0.0) + jnp.log(1.0 + jnp.exp(-jnp.abs(z))))

    def tail_in_block(log_keep):
        hi = log_keep.astype(BF16)
        lo = (log_keep - hi.astype(F32)).astype(BF16)
        return (jnp.dot(hi, later, preferred_element_type=F32)
                + jnp.dot(lo, later, preferred_element_type=F32))

    def weighted_values(j, w):
        return jnp.dot(w.astype(BF16), v_ref[0, pl.ds(j * blk, blk), :], preferred_element_type=F32)

    def first_two_blocks(q2, qi):
        prev = jnp.maximum(qi - 1, 0)
        z_d, log_keep_d_all = scores(q2, qi)
        z_p, log_keep_p = scores(q2, prev)
        log_keep_d = jnp.where(past, log_keep_d_all, 0.0)
        c_d = jnp.sum(log_keep_d, axis=-1, keepdims=True)
        w_d = jnp.where(past, jnp.exp(log_keep_d_all + z_d + tail_in_block(log_keep_d)), 0.0)
        w_p = jnp.exp(log_keep_p + z_p + (tail_in_block(log_keep_p) + c_d))
        w_p = jnp.where(qi > 0, w_p, 0.0)
        acc = weighted_values(qi, w_d) + weighted_values(prev, w_p)
        return c_d + jnp.sum(log_keep_p, axis=-1, keepdims=True), acc

    def earlier_blocks(q2, qi, c, acc):
        def more(carry):
            j, c, _ = carry
            return (j >= 0) & (jnp.max(c) > -SB_CUTOFF)

        def body(carry):
            j, c, acc = carry
            z, log_keep = scores(q2, j)
            w = jnp.exp(log_keep + z + (tail_in_block(log_keep) + c))
            return j - 1, c + jnp.sum(log_keep, axis=-1, keepdims=True), acc + weighted_values(j, w)

        return lax.while_loop(more, body, (qi - 2, c, acc))[2]

    blocks = []
    for sub in range(SB_TILE // blk):
        qi = pl.program_id(2) * (SB_TILE // blk) + sub
        q2 = _stack_heads(q_ref[0, sub * blk:(sub + 1) * blk, :])
        blocks.append((q2, qi) + first_two_blocks(q2, qi))
    for sub, (q2, qi, c, acc) in enumerate(blocks):
        acc = earlier_blocks(q2, qi, c, acc)
        o_ref[0, sub * blk:(sub + 1) * blk, :] = jnp.where(
            _head_lanes((blk, LANES), 0), acc[:blk], acc[blk:]).astype(o_ref.dtype)


def _sb_attention(proj, n_batch, seq):
    groups = SB_WIDTH // LANES
    return pl.pallas_call(
        _sb_kernel,
        grid=(n_batch, groups, seq // SB_TILE),
        in_specs=[pl.BlockSpec((1, SB_TILE, LANES), lambda b, g, i: (b, i, g)),
                  pl.BlockSpec((1, seq, LANES), lambda b, g, i: (b, 0, groups + g)),
                  pl.BlockSpec((1, seq, LANES), lambda b, g, i: (b, 0, 2 * groups + g))],
        out_specs=pl.BlockSpec((1, SB_TILE, LANES), lambda b, g, i: (b, i, g)),
        out_shape=jax.ShapeDtypeStruct((n_batch, seq, SB_WIDTH), BF16),
        compiler_params=_params(3),
        name="sb_attention",
    )(proj, proj, proj)


def _diff_kernel(q_ref, k_ref, v_ref, lq1_ref, lk1_ref, lq2_ref, lk2_ref, g_ref, o_ref,
                 s0_ref, s1_ref, m_ref, acc_ref, *, lam_init):
    head = pl.program_id(1)
    qi = pl.program_id(2)
    blk = DIFF_BLOCK
    tk = DIFF_CHUNK
    q2 = _stack_heads(q_ref[0])
    row = lax.broadcasted_iota(jnp.int32, (2 * blk, tk), 0) % blk
    col = lax.broadcasted_iota(jnp.int32, (2 * blk, tk), 1)

    past_scores = lambda j: _qk(q2, k_ref[0, pl.ds(j * tk, tk), :])

    _softmax_init(m_ref, acc_ref)
    for c in range(blk // tk):
        keys = pl.ds(qi * blk + c * tk, tk)
        s = jnp.where(col + c * tk <= row, _qk(q2, k_ref[0, keys, :]), -jnp.inf)
        if c == blk // tk - 1:
            _sweep_prefetch(past_scores, (s0_ref, s1_ref))
        _softmax_step(s, v_ref[0, keys, :], m_ref, acc_ref)

    _softmax_sweep(past_scores, lambda j: v_ref[0, pl.ds(j * tk, tk), :],
                   qi * (blk // tk), (s0_ref, s1_ref), m_ref, acc_ref)
    normed = _softmax_result(acc_ref)

    def lam_term(a_ref, b_ref):
        prod = a_ref[pl.ds(head, 1), :] * b_ref[pl.ds(head, 1), :]
        return jnp.exp(jnp.sum(prod, axis=-1, keepdims=True))

    lam = lam_term(lq1_ref, lk1_ref) - lam_term(lq2_ref, lk2_ref) + lam_init
    o = normed[:blk] - lam * normed[blk:]
    o_ref[0] = (_rms(o, g_ref[...]) * (1.0 - lam_init)).astype(o_ref.dtype)


def _diff_attention(proj, n_batch, seq, lq1, lk1, lq2, lk2, subln, lam_init):
    q0 = 3 * SB_WIDTH // LANES
    k0 = q0 + DIFF_WIDTH // LANES
    v0 = k0 + DIFF_WIDTH // LANES
    lam_spec = _resident((DIFF_HEADS, HEAD_DIM), lambda b, h, i: (0, 0))
    return pl.pallas_call(
        functools.partial(_diff_kernel, lam_init=lam_init),
        grid=(n_batch, DIFF_HEADS, seq // DIFF_BLOCK),
        in_specs=[pl.BlockSpec((1, DIFF_BLOCK, LANES), lambda b, h, i: (b, i, q0 + h)),
                  pl.BlockSpec((1, seq, LANES), lambda b, h, i: (b, 0, k0 + h)),
                  pl.BlockSpec((1, seq, LANES), lambda b, h, i: (b, 0, v0 + h)),
                  lam_spec, lam_spec, lam_spec, lam_spec,
                  _resident((1, LANES), lambda b, h, i: (0, 0))],
        out_specs=pl.BlockSpec((1, DIFF_BLOCK, LANES), lambda b, h, i: (b, i, h)),
        out_shape=jax.ShapeDtypeStruct((n_batch, seq, DIFF_WIDTH), BF16),
        scratch_shapes=[pltpu.VMEM((2 * DIFF_BLOCK, DIFF_CHUNK), F32),
                        pltpu.VMEM((2 * DIFF_BLOCK, DIFF_CHUNK), F32),
                        pltpu.VMEM((2 * DIFF_BLOCK, LANES), F32),
                        pltpu.VMEM((2 * DIFF_BLOCK, 2 * LANES), F32)],
        compiler_params=_params(3),
        name="diff_attention",
    )(proj, proj, proj, lq1, lk1, lq2, lk2, subln.reshape(1, LANES))


def _moba_kernel(q_ref, k_ref, v_ref, o_ref, kmean_ref, kext_ref, s0_ref, s1_ref, m_ref, acc_ref, *,
                 n_blocks):
    qi = pl.program_id(2)
    tile = MOBA_TILE
    seq = k_ref.shape[1]

    @pl.when(qi == 0)
    def _():
        kmean_ref[...] = jnp.zeros_like(kmean_ref)

        def fill_mean(b, carry):
            kb = k_ref[0, pl.ds(b * MOBA_BLOCK, MOBA_BLOCK), :].astype(F32)
            kmean_ref[pl.ds(b, 1), :] = jnp.sum(kb, axis=0, keepdims=True) * (1.0 / MOBA_BLOCK)
            return carry

        lax.fori_loop(0, n_blocks, fill_mean, 0)

        def fill_ext(c, carry):
            rows = pl.ds(c * tile, tile)
            key_blk = (c * tile + lax.broadcasted_iota(jnp.int32, (tile, LANES), 0)) // MOBA_BLOCK
            lane = lax.broadcasted_iota(jnp.int32, (tile, LANES), 1)
            kext_ref[rows, :LANES] = k_ref[0, rows, :]
            kext_ref[rows, LANES:] = jnp.where(lane == key_blk, 1.0, 0.0).astype(BF16)
            return carry

        lax.fori_loop(0, seq // tile, fill_ext, 0)

    q2 = _stack_heads(q_ref[0])

    km = kmean_ref[...]
    km_hi = km.astype(BF16)
    km_lo = (km - km_hi.astype(F32)).astype(BF16)
    gate = _qk(km_hi, q2) + _qk(km_lo, q2)
    rows = -(-n_blocks // 8) * 8
    q_pos = lax.broadcasted_iota(jnp.int32, (rows, 2 * tile), 1) % tile + qi * tile
    own_blk = q_pos // MOBA_BLOCK
    blk_id = lax.broadcasted_iota(jnp.int32, (rows, 2 * tile), 0)
    blk_f = blk_id.astype(F32)
    g = jnp.where(blk_id < own_blk, gate[:rows], -jnp.inf)
    chosen = blk_id == own_blk
    for _ in range(MOBA_TOPK):
        top = jnp.max(g, axis=0, keepdims=True)
        first = jnp.min(jnp.where(g == top, blk_f, float(2 * LANES)), axis=0, keepdims=True)
        pick = (blk_f == first) & (top > -jnp.inf)
        chosen = chosen | pick
        g = jnp.where(pick, -jnp.inf, g)
    bias = jnp.where(chosen, 0.0, MASK_BIAS)
    if rows < LANES:
        bias = jnp.concatenate([bias, jnp.zeros((LANES - rows, 2 * tile), F32)], axis=0)
    bias = bias.T.astype(BF16)
    q_ext = jnp.concatenate([q2, bias], axis=1)

    past_scores = lambda j: _qk(q_ext, kext_ref[pl.ds(j * tile, tile), :])

    row = lax.broadcasted_iota(jnp.int32, (2 * tile, tile), 0) % tile
    col = lax.broadcasted_iota(jnp.int32, (2 * tile, tile), 1)
    s = jnp.where(col <= row, _qk(q_ext, kext_ref[pl.ds(qi * tile, tile), :]), -jnp.inf)
    _sweep_prefetch(past_scores, (s0_ref, s1_ref))
    _softmax_init(m_ref, acc_ref)
    _softmax_step(s, v_ref[0, pl.ds(qi * tile, tile), :], m_ref, acc_ref)

    _softmax_sweep(past_scores, lambda j: v_ref[0, pl.ds(j * tile, tile), :],
                   qi, (s0_ref, s1_ref), m_ref, acc_ref)
    o = _softmax_result(acc_ref)
    o_ref[0] = jnp.where(_head_lanes((tile, LANES), 0), o[:tile], o[tile:]).astype(o_ref.dtype)


def _moba_attention(proj, n_batch, seq):
    groups = MOBA_WIDTH // LANES
    n_blocks = seq // MOBA_BLOCK
    assert seq % MOBA_TILE == 0 and n_blocks <= LANES, "the block choice rides on 128 bias lanes"
    return pl.pallas_call(
        functools.partial(_moba_kernel, n_blocks=n_blocks),
        grid=(n_batch, groups, seq // MOBA_TILE),
        in_specs=[pl.BlockSpec((1, MOBA_TILE, LANES), lambda b, g, i: (b, i, g)),
                  pl.BlockSpec((1, seq, LANES), lambda b, g, i: (b, 0, groups + g)),
                  pl.BlockSpec((1, seq, LANES), lambda b, g, i: (b, 0, 2 * groups + g))],
        out_specs=pl.BlockSpec((1, MOBA_TILE, LANES), lambda b, g, i: (b, i, g)),
        out_shape=jax.ShapeDtypeStruct((n_batch, seq, MOBA_WIDTH), BF16),
        scratch_shapes=[pltpu.VMEM((LANES, LANES), F32),
                        pltpu.VMEM((seq, 2 * LANES), BF16),
                        pltpu.VMEM((2 * MOBA_TILE, MOBA_TILE), F32),
                        pltpu.VMEM((2 * MOBA_TILE, MOBA_TILE), F32),
                        pltpu.VMEM((2 * MOBA_TILE, LANES), F32),
                        pltpu.VMEM((2 * MOBA_TILE, 2 * LANES), F32)],
        compiler_params=_params(3),
        name="moba_attention",
    )(proj, proj, proj)


def _post_attention_kernel(h_ref, a_ref, b_ref, p_ref, wa_ref, wb_ref, g_mlp_ref, w1_ref, w2_ref,
                           g_ple_ref, wg_ref, wp_ref, g_final_ref, o_ref, *, final_norm):
    h = (h_ref[...]
         + jnp.dot(a_ref[...], wa_ref[...], preferred_element_type=F32)
         + jnp.dot(b_ref[...], wb_ref[...], preferred_element_type=F32))
    u = _rms(h, g_mlp_ref[...]).astype(BF16)
    o_ref[...] = h
    for c in range(w1_ref.shape[1] // PROJ_CHUNK):
        cols = slice(c * PROJ_CHUNK, (c + 1) * PROJ_CHUNK)
        a = jnp.maximum(jnp.dot(u, w1_ref[:, cols], preferred_element_type=F32), 0.0)
        o_ref[...] += jnp.dot((a * a).astype(BF16), w2_ref[cols, :], preferred_element_type=F32)
    h = o_ref[...]
    u = _rms(h, g_ple_ref[...]).astype(BF16)
    gate = jax.nn.sigmoid(jnp.dot(u, wg_ref[...], preferred_element_type=F32))
    emb = jnp.dot(p_ref[...].astype(BF16), wp_ref[...], preferred_element_type=F32)
    out = h + gate * emb
    if final_norm:
        out = _rms(out, g_final_ref[...])
    o_ref[...] = out


def _post_attention(h, a, b, a_col, b_col, p, w_out, g_mlp, w1, w2, g_ple, wg, wp, g_final, final_norm):
    t, d = h.shape
    half = w_out.shape[0] // 2
    dp = p.shape[1]
    d_ff = w1.shape[1]
    tm = min(TOKEN_TILE, t)
    row = lambda i: (i, 0)
    fixed = lambda i: (0, 0)
    return pl.pallas_call(
        functools.partial(_post_attention_kernel, final_norm=final_norm),
        grid=(t // tm,),
        in_specs=[pl.BlockSpec((tm, d), row),
                  pl.BlockSpec((tm, half), lambda i: (i, a_col)),
                  pl.BlockSpec((tm, half), lambda i: (i, b_col)),
                  pl.BlockSpec((tm, dp), row),
                  _resident((half, d), fixed),
                  _resident((half, d), lambda i: (1, 0)),
                  _resident((1, d), fixed),
                  _resident((d, d_ff), fixed),
                  _resident((d_ff, d), fixed),
                  _resident((1, d), fixed),
                  _resident((d, d), fixed),
                  _resident((dp, d), fixed),
                  _resident((1, d), fixed)],
        out_specs=pl.BlockSpec((tm, d), row),
        out_shape=jax.ShapeDtypeStruct((t, d), F32),
        compiler_params=_params(1),
        name="post_attention",
    )(h, a, b, p, w_out, w_out, g_mlp.reshape(1, d), w1, w2, g_ple.reshape(1, d), wg, wp,
      g_final.reshape(1, d))


def kernel(x, p, positions, attn_norm, ab_w_in, ab_w_out, diff_lam_q1, diff_lam_k1, diff_lam_q2,
           diff_lam_k2, diff_subln, moba_w_in, moba_w_out, mlp_norm, w_ff1, w_ff2, ple_norm,
           ple_gate, ple_proj, final_norm):
    n_batch, seq, d = x.shape
    depth = p.shape[0]
    t = n_batch * seq
    assert seq % DIFF_BLOCK == 0 and seq % SB_TILE == 0 and t % TOKEN_TILE == 0
    cos_t, sin_t = _rope_tables(positions)
    h = x.reshape(t, d)
    even_rope, even_scale = (3, 4), (0, 3)
    odd_rope, odd_scale = (0, 1, 2, 3), (0, 1)
    for i in range(depth):
        j = i // 2
        if i % 2 == 0:
            proj = _norm_proj(h, attn_norm[i], ab_w_in[j].astype(BF16), cos_t, sin_t,
                              even_rope, even_scale).reshape(n_batch, seq, -1)
            lam_init = 0.8 - 0.6 * math.exp(-0.3 * i)
            o_a = _sb_attention(proj, n_batch, seq).reshape(t, SB_WIDTH)
            o_b = _diff_attention(proj, n_batch, seq, diff_lam_q1[j], diff_lam_k1[j], diff_lam_q2[j],
                                  diff_lam_k2[j], diff_subln[j], lam_init).reshape(t, DIFF_WIDTH)
            mixed, w_out = (o_a, o_b, 0, 0), ab_w_out[j]
        else:
            proj = _norm_proj(h, attn_norm[i], moba_w_in[j].astype(BF16), cos_t, sin_t,
                              odd_rope, odd_scale).reshape(n_batch, seq, -1)
            o = _moba_attention(proj, n_batch, seq).reshape(t, MOBA_WIDTH)
            mixed, w_out = (o, o, 0, 1), moba_w_out[j]
        h = _post_attention(h, *mixed, p[i].reshape(t, -1), w_out.astype(BF16), mlp_norm[i],
                            w_ff1[i].astype(BF16), w_ff2[i].astype(BF16), ple_norm[i],
                            ple_gate[i].astype(BF16), ple_proj[i].astype(BF16), final_norm,
                            i == depth - 1)
    return h.reshape(n_batch, seq, d)
```

```python
import functools
import math

import jax
import jax.numpy as jnp
from jax import lax
from jax.experimental import pallas as pl
from jax.experimental.pallas import tpu as pltpu

F32 = jnp.float32
BF16 = jnp.bfloat16

NORM_EPS = 1e-6
ROPE_THETA = 500000.0
HEAD_DIM = 64
ROPE_DIMS = HEAD_DIM // 4
ROPE_HALF = ROPE_DIMS // 2
LANES = 128
QK_SCALE = HEAD_DIM ** -0.5
SB_WIDTH = 512
DIFF_WIDTH = 512
DIFF_HEADS = 4
MOBA_WIDTH = 1024
MOBA_BLOCK = 256
MOBA_TOPK = 3
MOBA_TILE = 512
ATTN_BLOCK = 256
SB_TILE = 1024
DIFF_BLOCK = 512
DIFF_CHUNK = 512
SWEEP_UNROLL = 4
PROJ_CHUNK = 512
TOKEN_TILE = 512
MASK_BIAS = -1e30
SB_CUTOFF = 110.0
VMEM_LIMIT = 48 * 1024 * 1024


def _params(n_axes):
    return pltpu.CompilerParams(dimension_semantics=("arbitrary",) * n_axes,
                                vmem_limit_bytes=VMEM_LIMIT)


def _resident(shape, index_map):
    return pl.BlockSpec(shape, index_map, pipeline_mode=pl.Buffered(1))


def _rope_table_kernel(pos_ref, invf_ref, cos_ref, sin_ref):
    ang = pos_ref[...].astype(F32) * invf_ref[...]
    lane = lax.broadcasted_iota(jnp.int32, ang.shape, 1) % HEAD_DIM
    c = jnp.cos(ang)
    s = jnp.sin(ang)
    cos_ref[...] = jnp.where(lane < ROPE_DIMS, c, 1.0)
    sin_ref[...] = jnp.where(lane < ROPE_HALF, -s, jnp.where(lane < ROPE_DIMS, s, 0.0))


def _rope_tables(positions):
    t = positions.size
    pos = positions.reshape(t, 1)
    inv_freq = ROPE_THETA ** (-(jnp.arange(0, ROPE_DIMS, 2, dtype=F32) / ROPE_DIMS))
    invf = jnp.tile(inv_freq, LANES // ROPE_HALF).reshape(1, LANES)
    tm = min(2048, t)
    return pl.pallas_call(
        _rope_table_kernel,
        grid=(t // tm,),
        in_specs=[pl.BlockSpec((tm, 1), lambda i: (i, 0)),
                  pl.BlockSpec((1, LANES), lambda i: (0, 0))],
        out_specs=[pl.BlockSpec((tm, LANES), lambda i: (i, 0))] * 2,
        out_shape=[jax.ShapeDtypeStruct((t, LANES), F32)] * 2,
        compiler_params=_params(1),
        name="rope_tables",
    )(pos, invf)


def _rope(a, cos, sin):
    lane = lax.broadcasted_iota(jnp.int32, a.shape, 1) % HEAD_DIM
    partner = jnp.where(lane < ROPE_HALF,
                        pltpu.roll(a, LANES - ROPE_HALF, 1),
                        pltpu.roll(a, ROPE_HALF, 1))
    return a * cos + partner * sin


def _rms(x, g):
    ms = jnp.mean(x * x, axis=-1, keepdims=True)
    return x * lax.rsqrt(ms + NORM_EPS) * g


def _norm_proj_kernel(x_ref, g_ref, w_ref, cos_ref, sin_ref, o_ref, *, rope_chunks, scale_chunks):
    u = _rms(x_ref[...], g_ref[...]).astype(BF16)
    n = w_ref.shape[1]
    for c in range(n // PROJ_CHUNK):
        cols = slice(c * PROJ_CHUNK, (c + 1) * PROJ_CHUNK)
        acc = jnp.dot(u, w_ref[:, cols], preferred_element_type=F32)
        if c in scale_chunks:
            acc = acc * QK_SCALE
        if c in rope_chunks:
            cos = cos_ref[...]
            sin = sin_ref[...]
            for gidx in range(PROJ_CHUNK // LANES):
                lanes = slice(gidx * LANES, (gidx + 1) * LANES)
                o_ref[:, c * PROJ_CHUNK + gidx * LANES:c * PROJ_CHUNK + (gidx + 1) * LANES] = (
                    _rope(acc[:, lanes], cos, sin).astype(BF16))
        else:
            o_ref[:, cols] = acc.astype(BF16)


def _norm_proj(h, g, w, cos_t, sin_t, rope_chunks, scale_chunks):
    t, d = h.shape
    n = w.shape[1]
    tm = min(TOKEN_TILE, t)
    kern = functools.partial(_norm_proj_kernel, rope_chunks=rope_chunks, scale_chunks=scale_chunks)
    return pl.pallas_call(
        kern,
        grid=(t // tm,),
        in_specs=[pl.BlockSpec((tm, d), lambda i: (i, 0)),
                  _resident((1, d), lambda i: (0, 0)),
                  _resident((d, n), lambda i: (0, 0)),
                  pl.BlockSpec((tm, LANES), lambda i: (i, 0)),
                  pl.BlockSpec((tm, LANES), lambda i: (i, 0))],
        out_specs=pl.BlockSpec((tm, n), lambda i: (i, 0)),
        out_shape=jax.ShapeDtypeStruct((t, n), BF16),
        compiler_params=_params(1),
        name="norm_proj",
    )(h, g.reshape(1, d), w, cos_t, sin_t)


def _head_lanes(shape, head):
    lane = lax.broadcasted_iota(jnp.int32, shape, 1)
    return (lane >= head * HEAD_DIM) & (lane < (head + 1) * HEAD_DIM)


def _stack_heads(q):
    return jnp.concatenate([jnp.where(_head_lanes(q.shape, 0), q, jnp.zeros_like(q)),
                            jnp.where(_head_lanes(q.shape, 1), q, jnp.zeros_like(q))], axis=0)


def _qk(q, k):
    return lax.dot_general(q, k, (((1,), (1,)), ((), ())), preferred_element_type=F32)


def _softmax_init(m_ref, acc_ref):
    m_ref[...] = jnp.full(m_ref.shape, -jnp.inf, F32)
    acc_ref[...] = jnp.zeros(acc_ref.shape, F32)


def _softmax_step(s, v, m_ref, acc_ref):
    m = m_ref[...]
    m_new = jnp.maximum(m, jnp.max(s, axis=-1, keepdims=True))
    alpha = jnp.exp(m - m_new)
    alpha = jnp.concatenate([alpha, alpha], axis=1)
    p = jnp.concatenate([jnp.exp(s[:, c * LANES:(c + 1) * LANES] - m_new)
                         for c in range(s.shape[1] // LANES)], axis=1).astype(BF16)
    v_ones = jnp.concatenate([v, jnp.ones_like(v)], axis=1)
    half = p.shape[0] // 2
    acc_ref[:half] = alpha[:half] * acc_ref[:half] + jnp.dot(p[:half], v_ones, preferred_element_type=F32)
    acc_ref[half:] = alpha[half:] * acc_ref[half:] + jnp.dot(p[half:], v_ones, preferred_element_type=F32)
    m_ref[...] = m_new


def _softmax_sweep(score_fn, v_fn, n_chunks, s_refs, m_ref, acc_ref):
    unroll = SWEEP_UNROLL

    def run(first, count, then_prefetch):
        for u in range(count):
            if u + 1 < count or then_prefetch:
                s_refs[(u + 1) % 2][...] = score_fn(first + u + 1)
            _softmax_step(s_refs[u % 2][...], v_fn(first + u), m_ref, acc_ref)

    def body(i, carry):
        run(unroll * i, unroll, True)
        return carry

    n_iters = jnp.maximum((n_chunks - 1) // unroll, 0)
    lax.fori_loop(0, n_iters, body, 0)
    last = unroll * n_iters

    for remaining in range(1, unroll + 1):
        @pl.when((n_chunks > 0) & (n_chunks - last == remaining))
        def _(remaining=remaining):
            run(last, remaining, False)


def _sweep_prefetch(score_fn, s_refs):
    s_refs[0][...] = score_fn(0)


def _softmax_result(acc_ref):
    return acc_ref[:, :LANES] / acc_ref[:, LANES:]


def _sb_kernel(q_ref, k_ref, v_ref, o_ref):
    blk = ATTN_BLOCK
    row = lax.broadcasted_iota(jnp.int32, (2 * blk, blk), 0) % blk
    col = lax.broadcasted_iota(jnp.int32, (2 * blk, blk), 1)
    past = col < row
    later = (lax.broadcasted_iota(jnp.int32, (blk, blk), 0)
             > lax.broadcasted_iota(jnp.int32, (blk, blk), 1)).astype(BF16)

    def scores(q2, j):
        z = _qk(q2, k_ref[0, pl.ds(j * blk, blk), :])
        return z, -(jnp.maximum(z, 0.0) + jnp.log(1.0 + jnp.exp(-jnp.abs(z))))

    def tail_in_block(log_keep):
        hi = log_keep.astype(BF16)
        lo = (log_keep - hi.astype(F32)).astype(BF16)
        return (jnp.dot(hi, later, preferred_element_type=F32)
                + jnp.dot(lo, later, preferred_element_type=F32))

    def weighted_values(j, w):
        return jnp.dot(w.astype(BF16), v_ref[0, pl.ds(j * blk, blk), :], preferred_element_type=F32)

    def first_two_blocks(q2, qi):
        prev = jnp.maximum(qi - 1, 0)
        z_d, log_keep_d_all = scores(q2, qi)
        z_p, log_keep_p = scores(q2, prev)
        log_keep_d = jnp.where(past, log_keep_d_all, 0.0)
        c_d = jnp.sum(log_keep_d, axis=-1, keepdims=True)
        w_d = jnp.where(past, jnp.exp(log_keep_d_all + z_d + tail_in_block(log_keep_d)), 0.0)
        w_p = jnp.exp(log_keep_p + z_p + (tail_in_block(log_keep_p) + c_d))
        w_p = jnp.where(qi > 0, w_p, 0.0)
        acc = weighted_values(qi, w_d) + weighted_values(prev, w_p)
        return c_d + jnp.sum(log_keep_p, axis=-1, keepdims=True), acc

    def earlier_blocks(q2, qi, c, acc):
        def more(carry):
            j, c, _ = carry
            return (j >= 0) & (jnp.max(c) > -SB_CUTOFF)

        def body(carry):
            j, c, acc = carry
            z, log_keep = scores(q2, j)
            w = jnp.exp(log_keep + z + (tail_in_block(log_keep) + c))
            return j - 1, c + jnp.sum(log_keep, axis=-1, keepdims=True), acc + weighted_values(j, w)

        return lax.while_loop(more, body, (qi - 2, c, acc))[2]

    blocks = []
    for sub in range(SB_TILE // blk):
        qi = pl.program_id(2) * (SB_TILE // blk) + sub
        q2 = _stack_heads(q_ref[0, sub * blk:(sub + 1) * blk, :])
        blocks.append((q2, qi) + first_two_blocks(q2, qi))
    for sub, (q2, qi, c, acc) in enumerate(blocks):
        acc = earlier_blocks(q2, qi, c, acc)
        o_ref[0, sub * blk:(sub + 1) * blk, :] = jnp.where(
            _head_lanes((blk, LANES), 0), acc[:blk], acc[blk:]).astype(o_ref.dtype)


def _sb_attention(proj, n_batch, seq):
    groups = SB_WIDTH // LANES
    return pl.pallas_call(
        _sb_kernel,
        grid=(n_batch, groups, seq // SB_TILE),
        in_specs=[pl.BlockSpec((1, SB_TILE, LANES), lambda b, g, i: (b, i, g)),
                  pl.BlockSpec((1, seq, LANES), lambda b, g, i: (b, 0, groups + g)),
                  pl.BlockSpec((1, seq, LANES), lambda b, g, i: (b, 0, 2 * groups + g))],
        out_specs=pl.BlockSpec((1, SB_TILE, LANES), lambda b, g, i: (b, i, g)),
        out_shape=jax.ShapeDtypeStruct((n_batch, seq, SB_WIDTH), BF16),
        compiler_params=_params(3),
        name="sb_attention",
    )(proj, proj, proj)


def _diff_kernel(q_ref, k_ref, v_ref, lq1_ref, lk1_ref, lq2_ref, lk2_ref, g_ref, o_ref,
                 s0_ref, s1_ref, m_ref, acc_ref, *, lam_init):
    head = pl.program_id(1)
    qi = pl.program_id(2)
    blk = DIFF_BLOCK
    tk = DIFF_CHUNK
    q2 = _stack_heads(q_ref[0])
    row = lax.broadcasted_iota(jnp.int32, (2 * blk, tk), 0) % blk
    col = lax.broadcasted_iota(jnp.int32, (2 * blk, tk), 1)

    past_scores = lambda j: _qk(q2, k_ref[0, pl.ds(j * tk, tk), :])

    _softmax_init(m_ref, acc_ref)
    for c in range(blk // tk):
        keys = pl.ds(qi * blk + c * tk, tk)
        s = jnp.where(col + c * tk <= row, _qk(q2, k_ref[0, keys, :]), -jnp.inf)
        if c == blk // tk - 1:
            _sweep_prefetch(past_scores, (s0_ref, s1_ref))
        _softmax_step(s, v_ref[0, keys, :], m_ref, acc_ref)

    _softmax_sweep(past_scores, lambda j: v_ref[0, pl.ds(j * tk, tk), :],
                   qi * (blk // tk), (s0_ref, s1_ref), m_ref, acc_ref)
    normed = _softmax_result(acc_ref)

    def lam_term(a_ref, b_ref):
        prod = a_ref[pl.ds(head, 1), :] * b_ref[pl.ds(head, 1), :]
        return jnp.exp(jnp.sum(prod, axis=-1, keepdims=True))

    lam = lam_term(lq1_ref, lk1_ref) - lam_term(lq2_ref, lk2_ref) + lam_init
    o = normed[:blk] - lam * normed[blk:]
    o_ref[0] = (_rms(o, g_ref[...]) * (1.0 - lam_init)).astype(o_ref.dtype)


def _diff_attention(proj, n_batch, seq, lq1, lk1, lq2, lk2, subln, lam_init):
    q0 = 3 * SB_WIDTH // LANES
    k0 = q0 + DIFF_WIDTH // LANES
    v0 = k0 + DIFF_WIDTH // LANES
    lam_spec = _resident((DIFF_HEADS, HEAD_DIM), lambda b, h, i: (0, 0))
    return pl.pallas_call(
        functools.partial(_diff_kernel, lam_init=lam_init),
        grid=(n_batch, DIFF_HEADS, seq // DIFF_BLOCK),
        in_specs=[pl.BlockSpec((1, DIFF_BLOCK, LANES), lambda b, h, i: (b, i, q0 + h)),
                  pl.BlockSpec((1, seq, LANES), lambda b, h, i: (b, 0, k0 + h)),
                  pl.BlockSpec((1, seq, LANES), lambda b, h, i: (b, 0, v0 + h)),
                  lam_spec, lam_spec, lam_spec, lam_spec,
                  _resident((1, LANES), lambda b, h, i: (0, 0))],
        out_specs=pl.BlockSpec((1, DIFF_BLOCK, LANES), lambda b, h, i: (b, i, h)),
        out_shape=jax.ShapeDtypeStruct((n_batch, seq, DIFF_WIDTH), BF16),
        scratch_shapes=[pltpu.VMEM((2 * DIFF_BLOCK, DIFF_CHUNK), F32),
                        pltpu.VMEM((2 * DIFF_BLOCK, DIFF_CHUNK), F32),
                        pltpu.VMEM((2 * DIFF_BLOCK, LANES), F32),
                        pltpu.VMEM((2 * DIFF_BLOCK, 2 * LANES), F32)],
        compiler_params=_params(3),
        name="diff_attention",
    )(proj, proj, proj, lq1, lk1, lq2, lk2, subln.reshape(1, LANES))


def _moba_block_bias(q2, kmean_ref, tile_idx, n_blocks):
    n_rows = q2.shape[0]
    tile = n_rows // 2
    km = kmean_ref[...]
    km_hi = km.astype(BF16)
    km_lo = (km - km_hi.astype(F32)).astype(BF16)
    gate = _qk(km_hi, q2) + _qk(km_lo, q2)
    rows = -(-n_blocks // 8) * 8
    q_pos = lax.broadcasted_iota(jnp.int32, (rows, n_rows), 1) % tile + tile_idx * tile
    own_blk = q_pos // MOBA_BLOCK
    blk_id = lax.broadcasted_iota(jnp.int32, (rows, n_rows), 0)
    blk_f = blk_id.astype(F32)
    g = jnp.where(blk_id < own_blk, gate[:rows], -jnp.inf)
    chosen = blk_id == own_blk
    for _ in range(MOBA_TOPK):
        top = jnp.max(g, axis=0, keepdims=True)
        first = jnp.min(jnp.where(g == top, blk_f, float(2 * LANES)), axis=0, keepdims=True)
        pick = (blk_f == first) & (top > -jnp.inf)
        chosen = chosen | pick
        g = jnp.where(pick, -jnp.inf, g)
    bias = jnp.where(chosen, 0.0, MASK_BIAS)
    if rows < LANES:
        bias = jnp.concatenate([bias, jnp.zeros((LANES - rows, n_rows), F32)], axis=0)
    return bias.T.astype(BF16)


def _moba_kernel(q_ref, q_next_ref, k_ref, v_ref, o_ref, kmean_ref, kext_ref, bias_ref,
                 s0_ref, s1_ref, m_ref, acc_ref, *, n_blocks):
    qi = pl.program_id(2)
    tile = MOBA_TILE
    seq = k_ref.shape[1]

    @pl.when(qi == 0)
    def _():
        kmean_ref[...] = jnp.zeros_like(kmean_ref)

        def fill_mean(b, carry):
            kb = k_ref[0, pl.ds(b * MOBA_BLOCK, MOBA_BLOCK), :].astype(F32)
            kmean_ref[pl.ds(b, 1), :] = jnp.sum(kb, axis=0, keepdims=True) * (1.0 / MOBA_BLOCK)
            return carry

        lax.fori_loop(0, n_blocks, fill_mean, 0)

        def fill_ext(c, carry):
            rows = pl.ds(c * tile, tile)
            key_blk = (c * tile + lax.broadcasted_iota(jnp.int32, (tile, LANES), 0)) // MOBA_BLOCK
            lane = lax.broadcasted_iota(jnp.int32, (tile, LANES), 1)
            kext_ref[rows, :LANES] = k_ref[0, rows, :]
            kext_ref[rows, LANES:] = jnp.where(lane == key_blk, 1.0, 0.0).astype(BF16)
            return carry

        lax.fori_loop(0, seq // tile, fill_ext, 0)
        bias_ref[...] = _moba_block_bias(_stack_heads(q_ref[0]), kmean_ref, 0, n_blocks)

    q_ext = jnp.concatenate([_stack_heads(q_ref[0]), bias_ref[...]], axis=1)
    past_scores = lambda j: _qk(q_ext, kext_ref[pl.ds(j * tile, tile), :])
    bias_ref[...] = _moba_block_bias(_stack_heads(q_next_ref[0]), kmean_ref, qi + 1, n_blocks)

    row = lax.broadcasted_iota(jnp.int32, (2 * tile, tile), 0) % tile
    col = lax.broadcasted_iota(jnp.int32, (2 * tile, tile), 1)
    s = jnp.where(col <= row, _qk(q_ext, kext_ref[pl.ds(qi * tile, tile), :]), -jnp.inf)
    _sweep_prefetch(past_scores, (s0_ref, s1_ref))
    _softmax_init(m_ref, acc_ref)
    _softmax_step(s, v_ref[0, pl.ds(qi * tile, tile), :], m_ref, acc_ref)

    _softmax_sweep(past_scores, lambda j: v_ref[0, pl.ds(j * tile, tile), :],
                   qi, (s0_ref, s1_ref), m_ref, acc_ref)
    o = _softmax_result(acc_ref)
    o_ref[0] = jnp.where(_head_lanes((tile, LANES), 0), o[:tile], o[tile:]).astype(o_ref.dtype)


def _moba_attention(proj, n_batch, seq):
    groups = MOBA_WIDTH // LANES
    n_blocks = seq // MOBA_BLOCK
    n_tiles = seq // MOBA_TILE
    assert seq % MOBA_TILE == 0 and n_blocks <= LANES, "the block choice rides on 128 bias lanes"
    return pl.pallas_call(
        functools.partial(_moba_kernel, n_blocks=n_blocks),
        grid=(n_batch, groups, n_tiles),
        in_specs=[pl.BlockSpec((1, MOBA_TILE, LANES), lambda b, g, i: (b, i, g)),
                  pl.BlockSpec((1, MOBA_TILE, LANES),
                               lambda b, g, i: (b, jnp.minimum(i + 1, n_tiles - 1), g)),
                  pl.BlockSpec((1, seq, LANES), lambda b, g, i: (b, 0, groups + g)),
                  pl.BlockSpec((1, seq, LANES), lambda b, g, i: (b, 0, 2 * groups + g))],
        out_specs=pl.BlockSpec((1, MOBA_TILE, LANES), lambda b, g, i: (b, i, g)),
        out_shape=jax.ShapeDtypeStruct((n_batch, seq, MOBA_WIDTH), BF16),
        scratch_shapes=[pltpu.VMEM((LANES, LANES), F32),
                        pltpu.VMEM((seq, 2 * LANES), BF16),
                        pltpu.VMEM((2 * MOBA_TILE, LANES), BF16),
                        pltpu.VMEM((2 * MOBA_TILE, MOBA_TILE), F32),
                        pltpu.VMEM((2 * MOBA_TILE, MOBA_TILE), F32),
                        pltpu.VMEM((2 * MOBA_TILE, LANES), F32),
                        pltpu.VMEM((2 * MOBA_TILE, 2 * LANES), F32)],
        compiler_params=_params(3),
        name="moba_attention",
    )(proj, proj, proj, proj)


def _post_attention_kernel(h_ref, a_ref, b_ref, p_ref, wa_ref, wb_ref, g_mlp_ref, w1_ref, w2_ref,
                           g_ple_ref, wg_ref, wp_ref, g_final_ref, o_ref, *, final_norm):
    h = (h_ref[...]
         + jnp.dot(a_ref[...], wa_ref[...], preferred_element_type=F32)
         + jnp.dot(b_ref[...], wb_ref[...], preferred_element_type=F32))
    u = _rms(h, g_mlp_ref[...]).astype(BF16)
    o_ref[...] = h
    for c in range(w1_ref.shape[1] // PROJ_CHUNK):
        cols = slice(c * PROJ_CHUNK, (c + 1) * PROJ_CHUNK)
        a = jnp.maximum(jnp.dot(u, w1_ref[:, cols], preferred_element_type=F32), 0.0)
        o_ref[...] += jnp.dot((a * a).astype(BF16), w2_ref[cols, :], preferred_element_type=F32)
    h = o_ref[...]
    u = _rms(h, g_ple_ref[...]).astype(BF16)
    gate = jax.nn.sigmoid(jnp.dot(u, wg_ref[...], preferred_element_type=F32))
    emb = jnp.dot(p_ref[...].astype(BF16), wp_ref[...], preferred_element_type=F32)
    out = h + gate * emb
    if final_norm:
        out = _rms(out, g_final_ref[...])
    o_ref[...] = out


def _post_attention(h, a, b, a_col, b_col, p, w_out, g_mlp, w1, w2, g_ple, wg, wp, g_final, final_norm):
    t, d = h.shape
    half = w_out.shape[0] // 2
    dp = p.shape[1]
    d_ff = w1.shape[1]
    tm = min(TOKEN_TILE, t)
    row = lambda i: (i, 0)
    fixed = lambda i: (0, 0)
    return pl.pallas_call(
        functools.partial(_post_attention_kernel, final_norm=final_norm),
        grid=(t // tm,),
        in_specs=[pl.BlockSpec((tm, d), row),
                  pl.BlockSpec((tm, half), lambda i: (i, a_col)),
                  pl.BlockSpec((tm, half), lambda i: (i, b_col)),
                  pl.BlockSpec((tm, dp), row),
                  _resident((half, d), fixed),
                  _resident((half, d), lambda i: (1, 0)),
                  _resident((1, d), fixed),
                  _resident((d, d_ff), fixed),
                  _resident((d_ff, d), fixed),
                  _resident((1, d), fixed),
                  _resident((d, d), fixed),
                  _resident((dp, d), fixed),
                  _resident((1, d), fixed)],
        out_specs=pl.BlockSpec((tm, d), row),
        out_shape=jax.ShapeDtypeStruct((t, d), F32),
        compiler_params=_params(1),
        name="post_attention",
    )(h, a, b, p, w_out, w_out, g_mlp.reshape(1, d), w1, w2, g_ple.reshape(1, d), wg, wp,
      g_final.reshape(1, d))


def kernel(x, p, positions, attn_norm, ab_w_in, ab_w_out, diff_lam_q1, diff_lam_k1, diff_lam_q2,
           diff_lam_k2, diff_subln, moba_w_in, moba_w_out, mlp_norm, w_ff1, w_ff2, ple_norm,
           ple_gate, ple_proj, final_norm):
    n_batch, seq, d = x.shape
    depth = p.shape[0]
    t = n_batch * seq
    assert seq % DIFF_BLOCK == 0 and seq % SB_TILE == 0 and t % TOKEN_TILE == 0
    cos_t, sin_t = _rope_tables(positions)
    h = x.reshape(t, d)
    even_rope, even_scale = (3, 4), (0, 3)
    odd_rope, odd_scale = (0, 1, 2, 3), (0, 1)
    for i in range(depth):
        j = i // 2
        if i % 2 == 0:
            proj = _norm_proj(h, attn_norm[i], ab_w_in[j].astype(BF16), cos_t, sin_t,
                              even_rope, even_scale).reshape(n_batch, seq, -1)
            lam_init = 0.8 - 0.6 * math.exp(-0.3 * i)
            o_a = _sb_attention(proj, n_batch, seq).reshape(t, SB_WIDTH)
            o_b = _diff_attention(proj, n_batch, seq, diff_lam_q1[j], diff_lam_k1[j], diff_lam_q2[j],
                                  diff_lam_k2[j], diff_subln[j], lam_init).reshape(t, DIFF_WIDTH)
            mixed, w_out = (o_a, o_b, 0, 0), ab_w_out[j]
        else:
            proj = _norm_proj(h, attn_norm[i], moba_w_in[j].astype(BF16), cos_t, sin_t,
                              odd_rope, odd_scale).reshape(n_batch, seq, -1)
            o = _moba_attention(proj, n_batch, seq).reshape(t, MOBA_WIDTH)
            mixed, w_out = (o, o, 0, 1), moba_w_out[j]
        h = _post_attention(h, *mixed, p[i].reshape(t, -1), w_out.astype(BF16), mlp_norm[i],
                            w_ff1[i].astype(BF16), w_ff2[i].astype(BF16), ple_norm[i],
                            ple_gate[i].astype(BF16), ple_proj[i].astype(BF16), final_norm,
                            i == depth - 1)
    return h.reshape(n_batch, seq, d)
```

```python
import functools
import math

import jax
import jax.numpy as jnp
from jax import lax
from jax.experimental import pallas as pl
from jax.experimental.pallas import tpu as pltpu

F32 = jnp.float32
BF16 = jnp.bfloat16

NORM_EPS = 1e-6
ROPE_THETA = 500000.0
HEAD_DIM = 64
ROPE_DIMS = HEAD_DIM // 4
ROPE_HALF = ROPE_DIMS // 2
LANES = 128
QK_SCALE = HEAD_DIM ** -0.5
SB_WIDTH = 512
DIFF_WIDTH = 512
DIFF_HEADS = 4
MOBA_WIDTH = 1024
MOBA_BLOCK = 256
MOBA_TOPK = 3
MOBA_TILE = 512
ATTN_BLOCK = 256
SB_TILE = 1024
DIFF_BLOCK = 512
DIFF_CHUNK = 512
SWEEP_UNROLL = 4
PROJ_CHUNK = 512
TOKEN_TILE = 512
MASK_BIAS = -1e30
SB_CUTOFF = 110.0
VMEM_LIMIT = 48 * 1024 * 1024


def _params(n_axes):
    return pltpu.CompilerParams(dimension_semantics=("arbitrary",) * n_axes,
                                vmem_limit_bytes=VMEM_LIMIT)


def _resident(shape, index_map):
    return pl.BlockSpec(shape, index_map, pipeline_mode=pl.Buffered(1))


def _rope_table_kernel(pos_ref, invf_ref, cos_ref, sin_ref):
    ang = pos_ref[...].astype(F32) * invf_ref[...]
    lane = lax.broadcasted_iota(jnp.int32, ang.shape, 1) % HEAD_DIM
    c = jnp.cos(ang)
    s = jnp.sin(ang)
    cos_ref[...] = jnp.where(lane < ROPE_DIMS, c, 1.0)
    sin_ref[...] = jnp.where(lane < ROPE_HALF, -s, jnp.where(lane < ROPE_DIMS, s, 0.0))


def _rope_tables(positions):
    t = positions.size
    pos = positions.reshape(t, 1)
    inv_freq = ROPE_THETA ** (-(jnp.arange(0, ROPE_DIMS, 2, dtype=F32) / ROPE_DIMS))
    invf = jnp.tile(inv_freq, LANES // ROPE_HALF).reshape(1, LANES)
    tm = min(2048, t)
    return pl.pallas_call(
        _rope_table_kernel,
        grid=(t // tm,),
        in_specs=[pl.BlockSpec((tm, 1), lambda i: (i, 0)),
                  pl.BlockSpec((1, LANES), lambda i: (0, 0))],
        out_specs=[pl.BlockSpec((tm, LANES), lambda i: (i, 0))] * 2,
        out_shape=[jax.ShapeDtypeStruct((t, LANES), F32)] * 2,
        compiler_params=_params(1),
        name="rope_tables",
    )(pos, invf)


def _rope(a, cos, sin):
    lane = lax.broadcasted_iota(jnp.int32, a.shape, 1) % HEAD_DIM
    partner = jnp.where(lane < ROPE_HALF,
                        pltpu.roll(a, LANES - ROPE_HALF, 1),
                        pltpu.roll(a, ROPE_HALF, 1))
    return a * cos + partner * sin


def _rms(x, g):
    ms = jnp.mean(x * x, axis=-1, keepdims=True)
    return x * lax.rsqrt(ms + NORM_EPS) * g


def _norm_proj_kernel(x_ref, g_ref, w_ref, cos_ref, sin_ref, o_ref, *, rope_chunks, scale_chunks):
    u = _rms(x_ref[...], g_ref[...]).astype(BF16)
    n = w_ref.shape[1]
    for c in range(n // PROJ_CHUNK):
        cols = slice(c * PROJ_CHUNK, (c + 1) * PROJ_CHUNK)
        acc = jnp.dot(u, w_ref[:, cols], preferred_element_type=F32)
        if c in scale_chunks:
            acc = acc * QK_SCALE
        if c in rope_chunks:
            cos = cos_ref[...]
            sin = sin_ref[...]
            for gidx in range(PROJ_CHUNK // LANES):
                lanes = slice(gidx * LANES, (gidx + 1) * LANES)
                o_ref[:, c * PROJ_CHUNK + gidx * LANES:c * PROJ_CHUNK + (gidx + 1) * LANES] = (
                    _rope(acc[:, lanes], cos, sin).astype(BF16))
        else:
            o_ref[:, cols] = acc.astype(BF16)


def _norm_proj(h, g, w, cos_t, sin_t, rope_chunks, scale_chunks):
    t, d = h.shape
    n = w.shape[1]
    tm = min(TOKEN_TILE, t)
    kern = functools.partial(_norm_proj_kernel, rope_chunks=rope_chunks, scale_chunks=scale_chunks)
    return pl.pallas_call(
        kern,
        grid=(t // tm,),
        in_specs=[pl.BlockSpec((tm, d), lambda i: (i, 0)),
                  _resident((1, d), lambda i: (0, 0)),
                  _resident((d, n), lambda i: (0, 0)),
                  pl.BlockSpec((tm, LANES), lambda i: (i, 0)),
                  pl.BlockSpec((tm, LANES), lambda i: (i, 0))],
        out_specs=pl.BlockSpec((tm, n), lambda i: (i, 0)),
        out_shape=jax.ShapeDtypeStruct((t, n), BF16),
        compiler_params=_params(1),
        name="norm_proj",
    )(h, g.reshape(1, d), w, cos_t, sin_t)


def _head_lanes(shape, head):
    lane = lax.broadcasted_iota(jnp.int32, shape, 1)
    return (lane >= head * HEAD_DIM) & (lane < (head + 1) * HEAD_DIM)


def _stack_heads(q):
    return jnp.concatenate([jnp.where(_head_lanes(q.shape, 0), q, jnp.zeros_like(q)),
                            jnp.where(_head_lanes(q.shape, 1), q, jnp.zeros_like(q))], axis=0)


def _qk(q, k):
    return lax.dot_general(q, k, (((1,), (1,)), ((), ())), preferred_element_type=F32)


def _softmax_init(m_ref, acc_ref):
    m_ref[...] = jnp.full(m_ref.shape, -jnp.inf, F32)
    acc_ref[...] = jnp.zeros(acc_ref.shape, F32)


def _softmax_step(s, v, m_ref, acc_ref):
    m = m_ref[...]
    m_new = jnp.maximum(m, jnp.max(s, axis=-1, keepdims=True))
    alpha = jnp.exp(m - m_new)
    alpha = jnp.concatenate([alpha, alpha], axis=1)
    p = jnp.concatenate([jnp.exp(s[:, c * LANES:(c + 1) * LANES] - m_new)
                         for c in range(s.shape[1] // LANES)], axis=1).astype(BF16)
    v_ones = jnp.concatenate([v, jnp.ones_like(v)], axis=1)
    half = p.shape[0] // 2
    acc_ref[:half] = alpha[:half] * acc_ref[:half] + jnp.dot(p[:half], v_ones, preferred_element_type=F32)
    acc_ref[half:] = alpha[half:] * acc_ref[half:] + jnp.dot(p[half:], v_ones, preferred_element_type=F32)
    m_ref[...] = m_new


def _softmax_sweep(score_fn, v_fn, n_chunks, s_refs, m_ref, acc_ref):
    unroll = SWEEP_UNROLL

    def run(first, count, then_prefetch):
        for u in range(count):
            if u + 1 < count or then_prefetch:
                s_refs[(u + 1) % 2][...] = score_fn(first + u + 1)
            _softmax_step(s_refs[u % 2][...], v_fn(first + u), m_ref, acc_ref)

    def body(i, carry):
        run(unroll * i, unroll, True)
        return carry

    n_iters = jnp.maximum((n_chunks - 1) // unroll, 0)
    lax.fori_loop(0, n_iters, body, 0)
    last = unroll * n_iters

    for remaining in range(1, unroll + 1):
        @pl.when((n_chunks > 0) & (n_chunks - last == remaining))
        def _(remaining=remaining):
            run(last, remaining, False)


def _sweep_prefetch(score_fn, s_refs):
    s_refs[0][...] = score_fn(0)


def _softmax_result(acc_ref):
    return acc_ref[:, :LANES] / acc_ref[:, LANES:]


def _sb_kernel(q_ref, k_ref, v_ref, o_ref):
    blk = ATTN_BLOCK
    row = lax.broadcasted_iota(jnp.int32, (2 * blk, blk), 0) % blk
    col = lax.broadcasted_iota(jnp.int32, (2 * blk, blk), 1)
    past = col < row
    later = (lax.broadcasted_iota(jnp.int32, (blk, blk), 0)
             > lax.broadcasted_iota(jnp.int32, (blk, blk), 1)).astype(BF16)

    def scores(q2, j):
        z = _qk(q2, k_ref[0, pl.ds(j * blk, blk), :])
        return z, -(jnp.maximum(z, 0.0) + jnp.log(1.0 + jnp.exp(-jnp.abs(z))))

    def tail_in_block(log_keep):
        return jnp.dot(log_keep.astype(BF16), later, preferred_element_type=F32)

    def weighted_values(j, w):
        return jnp.dot(w.astype(BF16), v_ref[0, pl.ds(j * blk, blk), :], preferred_element_type=F32)

    def first_two_blocks(q2, qi):
        prev = jnp.maximum(qi - 1, 0)
        z_d, log_keep_d_all = scores(q2, qi)
        z_p, log_keep_p = scores(q2, prev)
        log_keep_d = jnp.where(past, log_keep_d_all, 0.0)
        c_d = jnp.sum(log_keep_d, axis=-1, keepdims=True)
        w_d = jnp.where(past, jnp.exp(log_keep_d_all + z_d + tail_in_block(log_keep_d)), 0.0)
        w_p = jnp.exp(log_keep_p + z_p + (tail_in_block(log_keep_p) + c_d))
        w_p = jnp.where(qi > 0, w_p, 0.0)
        acc = weighted_values(qi, w_d) + weighted_values(prev, w_p)
        return c_d + jnp.sum(log_keep_p, axis=-1, keepdims=True), acc

    def earlier_blocks(q2, qi, c, acc):
        def more(carry):
            j, c, _ = carry
            return (j >= 0) & (jnp.max(c) > -SB_CUTOFF)

        def body(carry):
            j, c, acc = carry
            z, log_keep = scores(q2, j)
            w = jnp.exp(log_keep + z + (tail_in_block(log_keep) + c))
            return j - 1, c + jnp.sum(log_keep, axis=-1, keepdims=True), acc + weighted_values(j, w)

        return lax.while_loop(more, body, (qi - 2, c, acc))[2]

    blocks = []
    for sub in range(SB_TILE // blk):
        qi = pl.program_id(2) * (SB_TILE // blk) + sub
        q2 = _stack_heads(q_ref[0, sub * blk:(sub + 1) * blk, :])
        blocks.append((q2, qi) + first_two_blocks(q2, qi))
    for sub, (q2, qi, c, acc) in enumerate(blocks):
        acc = earlier_blocks(q2, qi, c, acc)
        o_ref[0, sub * blk:(sub + 1) * blk, :] = jnp.where(
            _head_lanes((blk, LANES), 0), acc[:blk], acc[blk:]).astype(o_ref.dtype)


def _sb_attention(proj, n_batch, seq):
    groups = SB_WIDTH // LANES
    return pl.pallas_call(
        _sb_kernel,
        grid=(n_batch, groups, seq // SB_TILE),
        in_specs=[pl.BlockSpec((1, SB_TILE, LANES), lambda b, g, i: (b, i, g)),
                  pl.BlockSpec((1, seq, LANES), lambda b, g, i: (b, 0, groups + g)),
                  pl.BlockSpec((1, seq, LANES), lambda b, g, i: (b, 0, 2 * groups + g))],
        out_specs=pl.BlockSpec((1, SB_TILE, LANES), lambda b, g, i: (b, i, g)),
        out_shape=jax.ShapeDtypeStruct((n_batch, seq, SB_WIDTH), BF16),
        compiler_params=_params(3),
        name="sb_attention",
    )(proj, proj, proj)


def _diff_kernel(q_ref, k_ref, v_ref, lq1_ref, lk1_ref, lq2_ref, lk2_ref, g_ref, o_ref,
                 s0_ref, s1_ref, m_ref, acc_ref, *, lam_init):
    head = pl.program_id(1)
    qi = pl.program_id(2)
    blk = DIFF_BLOCK
    tk = DIFF_CHUNK
    q2 = _stack_heads(q_ref[0])
    row = lax.broadcasted_iota(jnp.int32, (2 * blk, tk), 0) % blk
    col = lax.broadcasted_iota(jnp.int32, (2 * blk, tk), 1)

    past_scores = lambda j: _qk(q2, k_ref[0, pl.ds(j * tk, tk), :])

    _softmax_init(m_ref, acc_ref)
    for c in range(blk // tk):
        keys = pl.ds(qi * blk + c * tk, tk)
        s = jnp.where(col + c * tk <= row, _qk(q2, k_ref[0, keys, :]), -jnp.inf)
        if c == blk // tk - 1:
            _sweep_prefetch(past_scores, (s0_ref, s1_ref))
        _softmax_step(s, v_ref[0, keys, :], m_ref, acc_ref)

    _softmax_sweep(past_scores, lambda j: v_ref[0, pl.ds(j * tk, tk), :],
                   qi * (blk // tk), (s0_ref, s1_ref), m_ref, acc_ref)
    normed = _softmax_result(acc_ref)

    def lam_term(a_ref, b_ref):
        prod = a_ref[pl.ds(head, 1), :] * b_ref[pl.ds(head, 1), :]
        return jnp.exp(jnp.sum(prod, axis=-1, keepdims=True))

    lam = lam_term(lq1_ref, lk1_ref) - lam_term(lq2_ref, lk2_ref) + lam_init
    o = normed[:blk] - lam * normed[blk:]
    o_ref[0] = (_rms(o, g_ref[...]) * (1.0 - lam_init)).astype(o_ref.dtype)


def _diff_attention(proj, n_batch, seq, lq1, lk1, lq2, lk2, subln, lam_init):
    q0 = 3 * SB_WIDTH // LANES
    k0 = q0 + DIFF_WIDTH // LANES
    v0 = k0 + DIFF_WIDTH // LANES
    lam_spec = _resident((DIFF_HEADS, HEAD_DIM), lambda b, h, i: (0, 0))
    return pl.pallas_call(
        functools.partial(_diff_kernel, lam_init=lam_init),
        grid=(n_batch, DIFF_HEADS, seq // DIFF_BLOCK),
        in_specs=[pl.BlockSpec((1, DIFF_BLOCK, LANES), lambda b, h, i: (b, i, q0 + h)),
                  pl.BlockSpec((1, seq, LANES), lambda b, h, i: (b, 0, k0 + h)),
                  pl.BlockSpec((1, seq, LANES), lambda b, h, i: (b, 0, v0 + h)),
                  lam_spec, lam_spec, lam_spec, lam_spec,
                  _resident((1, LANES), lambda b, h, i: (0, 0))],
        out_specs=pl.BlockSpec((1, DIFF_BLOCK, LANES), lambda b, h, i: (b, i, h)),
        out_shape=jax.ShapeDtypeStruct((n_batch, seq, DIFF_WIDTH), BF16),
        scratch_shapes=[pltpu.VMEM((2 * DIFF_BLOCK, DIFF_CHUNK), F32),
                        pltpu.VMEM((2 * DIFF_BLOCK, DIFF_CHUNK), F32),
                        pltpu.VMEM((2 * DIFF_BLOCK, LANES), F32),
                        pltpu.VMEM((2 * DIFF_BLOCK, 2 * LANES), F32)],
        compiler_params=_params(3),
        name="diff_attention",
    )(proj, proj, proj, lq1, lk1, lq2, lk2, subln.reshape(1, LANES))


def _moba_block_bias(q2, kmean_ref, tile_idx, n_blocks):
    n_rows = q2.shape[0]
    tile = n_rows // 2
    km = kmean_ref[...]
    km_hi = km.astype(BF16)
    km_lo = (km - km_hi.astype(F32)).astype(BF16)
    gate = _qk(km_hi, q2) + _qk(km_lo, q2)
    rows = -(-n_blocks // 8) * 8
    q_pos = lax.broadcasted_iota(jnp.int32, (rows, n_rows), 1) % tile + tile_idx * tile
    own_blk = q_pos // MOBA_BLOCK
    blk_id = lax.broadcasted_iota(jnp.int32, (rows, n_rows), 0)
    blk_f = blk_id.astype(F32)
    g = jnp.where(blk_id < own_blk, gate[:rows], -jnp.inf)
    chosen = blk_id == own_blk
    for _ in range(MOBA_TOPK):
        top = jnp.max(g, axis=0, keepdims=True)
        first = jnp.min(jnp.where(g == top, blk_f, float(2 * LANES)), axis=0, keepdims=True)
        pick = (blk_f == first) & (top > -jnp.inf)
        chosen = chosen | pick
        g = jnp.where(pick, -jnp.inf, g)
    bias = jnp.where(chosen, 0.0, MASK_BIAS)
    if rows < LANES:
        bias = jnp.concatenate([bias, jnp.zeros((LANES - rows, n_rows), F32)], axis=0)
    return bias.T.astype(BF16)


def _moba_kernel(q_ref, q_next_ref, k_ref, v_ref, o_ref, kmean_ref, kext_ref, bias_ref,
                 s0_ref, s1_ref, m_ref, acc_ref, *, n_blocks):
    qi = pl.program_id(2)
    tile = MOBA_TILE
    seq = k_ref.shape[1]

    @pl.when(qi == 0)
    def _():
        kmean_ref[...] = jnp.zeros_like(kmean_ref)

        def fill_mean(b, carry):
            kb = k_ref[0, pl.ds(b * MOBA_BLOCK, MOBA_BLOCK), :].astype(F32)
            kmean_ref[pl.ds(b, 1), :] = jnp.sum(kb, axis=0, keepdims=True) * (1.0 / MOBA_BLOCK)
            return carry

        lax.fori_loop(0, n_blocks, fill_mean, 0)

        def fill_ext(c, carry):
            rows = pl.ds(c * tile, tile)
            key_blk = (c * tile + lax.broadcasted_iota(jnp.int32, (tile, LANES), 0)) // MOBA_BLOCK
            lane = lax.broadcasted_iota(jnp.int32, (tile, LANES), 1)
            kext_ref[rows, :LANES] = k_ref[0, rows, :]
            kext_ref[rows, LANES:] = jnp.where(lane == key_blk, 1.0, 0.0).astype(BF16)
            return carry

        lax.fori_loop(0, seq // tile, fill_ext, 0)
        bias_ref[...] = _moba_block_bias(_stack_heads(q_ref[0]), kmean_ref, 0, n_blocks)

    q_ext = jnp.concatenate([_stack_heads(q_ref[0]), bias_ref[...]], axis=1)
    past_scores = lambda j: _qk(q_ext, kext_ref[pl.ds(j * tile, tile), :])
    bias_ref[...] = _moba_block_bias(_stack_heads(q_next_ref[0]), kmean_ref, qi + 1, n_blocks)

    row = lax.broadcasted_iota(jnp.int32, (2 * tile, tile), 0) % tile
    col = lax.broadcasted_iota(jnp.int32, (2 * tile, tile), 1)
    s = jnp.where(col <= row, _qk(q_ext, kext_ref[pl.ds(qi * tile, tile), :]), -jnp.inf)
    _sweep_prefetch(past_scores, (s0_ref, s1_ref))
    _softmax_init(m_ref, acc_ref)
    _softmax_step(s, v_ref[0, pl.ds(qi * tile, tile), :], m_ref, acc_ref)

    _softmax_sweep(past_scores, lambda j: v_ref[0, pl.ds(j * tile, tile), :],
                   qi, (s0_ref, s1_ref), m_ref, acc_ref)
    o = _softmax_result(acc_ref)
    o_ref[0] = jnp.where(_head_lanes((tile, LANES), 0), o[:tile], o[tile:]).astype(o_ref.dtype)


def _moba_attention(proj, n_batch, seq):
    groups = MOBA_WIDTH // LANES
    n_blocks = seq // MOBA_BLOCK
    n_tiles = seq // MOBA_TILE
    assert seq % MOBA_TILE == 0 and n_blocks <= LANES, "the block choice rides on 128 bias lanes"
    return pl.pallas_call(
        functools.partial(_moba_kernel, n_blocks=n_blocks),
        grid=(n_batch, groups, n_tiles),
        in_specs=[pl.BlockSpec((1, MOBA_TILE, LANES), lambda b, g, i: (b, i, g)),
                  pl.BlockSpec((1, MOBA_TILE, LANES),
                               lambda b, g, i: (b, jnp.minimum(i + 1, n_tiles - 1), g)),
                  pl.BlockSpec((1, seq, LANES), lambda b, g, i: (b, 0, groups + g)),
                  pl.BlockSpec((1, seq, LANES), lambda b, g, i: (b, 0, 2 * groups + g))],
        out_specs=pl.BlockSpec((1, MOBA_TILE, LANES), lambda b, g, i: (b, i, g)),
        out_shape=jax.ShapeDtypeStruct((n_batch, seq, MOBA_WIDTH), BF16),
        scratch_shapes=[pltpu.VMEM((LANES, LANES), F32),
                        pltpu.VMEM((seq, 2 * LANES), BF16),
                        pltpu.VMEM((2 * MOBA_TILE, LANES), BF16),
                        pltpu.VMEM((2 * MOBA_TILE, MOBA_TILE), F32),
                        pltpu.VMEM((2 * MOBA_TILE, MOBA_TILE), F32),
                        pltpu.VMEM((2 * MOBA_TILE, LANES), F32),
                        pltpu.VMEM((2 * MOBA_TILE, 2 * LANES), F32)],
        compiler_params=_params(3),
        name="moba_attention",
    )(proj, proj, proj, proj)


def _post_attention_kernel(h_ref, a_ref, b_ref, p_ref, wa_ref, wb_ref, g_mlp_ref, w1_ref, w2_ref,
                           g_ple_ref, wg_ref, wp_ref, g_final_ref, o_ref, *, final_norm):
    h = (h_ref[...]
         + jnp.dot(a_ref[...], wa_ref[...], preferred_element_type=F32)
         + jnp.dot(b_ref[...], wb_ref[...], preferred_element_type=F32))
    u = _rms(h, g_mlp_ref[...]).astype(BF16)
    o_ref[...] = h
    for c in range(w1_ref.shape[1] // PROJ_CHUNK):
        cols = slice(c * PROJ_CHUNK, (c + 1) * PROJ_CHUNK)
        a = jnp.maximum(jnp.dot(u, w1_ref[:, cols], preferred_element_type=F32), 0.0)
        o_ref[...] += jnp.dot((a * a).astype(BF16), w2_ref[cols, :], preferred_element_type=F32)
    h = o_ref[...]
    u = _rms(h, g_ple_ref[...]).astype(BF16)
    gate = jax.nn.sigmoid(jnp.dot(u, wg_ref[...], preferred_element_type=F32))
    emb = jnp.dot(p_ref[...].astype(BF16), wp_ref[...], preferred_element_type=F32)
    out = h + gate * emb
    if final_norm:
        out = _rms(out, g_final_ref[...])
    o_ref[...] = out


def _post_attention(h, a, b, a_col, b_col, p, w_out, g_mlp, w1, w2, g_ple, wg, wp, g_final, final_norm):
    t, d = h.shape
    half = w_out.shape[0] // 2
    dp = p.shape[1]
    d_ff = w1.shape[1]
    tm = min(TOKEN_TILE, t)
    row = lambda i: (i, 0)
    fixed = lambda i: (0, 0)
    return pl.pallas_call(
        functools.partial(_post_attention_kernel, final_norm=final_norm),
        grid=(t // tm,),
        in_specs=[pl.BlockSpec((tm, d), row),
                  pl.BlockSpec((tm, half), lambda i: (i, a_col)),
                  pl.BlockSpec((tm, half), lambda i: (i, b_col)),
                  pl.BlockSpec((tm, dp), row),
                  _resident((half, d), fixed),
                  _resident((half, d), lambda i: (1, 0)),
                  _resident((1, d), fixed),
                  _resident((d, d_ff), fixed),
                  _resident((d_ff, d), fixed),
                  _resident((1, d), fixed),
                  _resident((d, d), fixed),
                  _resident((dp, d), fixed),
                  _resident((1, d), fixed)],
        out_specs=pl.BlockSpec((tm, d), row),
        out_shape=jax.ShapeDtypeStruct((t, d), F32),
        compiler_params=_params(1),
        name="post_attention",
    )(h, a, b, p, w_out, w_out, g_mlp.reshape(1, d), w1, w2, g_ple.reshape(1, d), wg, wp,
      g_final.reshape(1, d))


def kernel(x, p, positions, attn_norm, ab_w_in, ab_w_out, diff_lam_q1, diff_lam_k1, diff_lam_q2,
           diff_lam_k2, diff_subln, moba_w_in, moba_w_out, mlp_norm, w_ff1, w_ff2, ple_norm,
           ple_gate, ple_proj, final_norm):
    n_batch, seq, d = x.shape
    depth = p.shape[0]
    t = n_batch * seq
    assert seq % DIFF_BLOCK == 0 and seq % SB_TILE == 0 and t % TOKEN_TILE == 0
    cos_t, sin_t = _rope_tables(positions)
    h = x.reshape(t, d)
    even_rope, even_scale = (3, 4), (0, 3)
    odd_rope, odd_scale = (0, 1, 2, 3), (0, 1)
    for i in range(depth):
        j = i // 2
        if i % 2 == 0:
            proj = _norm_proj(h, attn_norm[i], ab_w_in[j].astype(BF16), cos_t, sin_t,
                              even_rope, even_scale).reshape(n_batch, seq, -1)
            lam_init = 0.8 - 0.6 * math.exp(-0.3 * i)
            o_a = _sb_attention(proj, n_batch, seq).reshape(t, SB_WIDTH)
            o_b = _diff_attention(proj, n_batch, seq, diff_lam_q1[j], diff_lam_k1[j], diff_lam_q2[j],
                                  diff_lam_k2[j], diff_subln[j], lam_init).reshape(t, DIFF_WIDTH)
            mixed, w_out = (o_a, o_b, 0, 0), ab_w_out[j]
        else:
            proj = _norm_proj(h, attn_norm[i], moba_w_in[j].astype(BF16), cos_t, sin_t,
                              odd_rope, odd_scale).reshape(n_batch, seq, -1)
            o = _moba_attention(proj, n_batch, seq).reshape(t, MOBA_WIDTH)
            mixed, w_out = (o, o, 0, 1), moba_w_out[j]
        h = _post_attention(h, *mixed, p[i].reshape(t, -1), w_out.astype(BF16), mlp_norm[i],
                            w_ff1[i].astype(BF16), w_ff2[i].astype(BF16), ple_norm[i],
                            ple_gate[i].astype(BF16), ple_proj[i].astype(BF16), final_norm,
                            i == depth - 1)
    return h.reshape(n_batch, seq, d)
```

```python
import functools
import math

import jax
import jax.numpy as jnp
from jax import lax
from jax.experimental import pallas as pl
from jax.experimental.pallas import tpu as pltpu

F32 = jnp.float32
BF16 = jnp.bfloat16

NORM_EPS = 1e-6
ROPE_THETA = 500000.0
HEAD_DIM = 64
ROPE_DIMS = HEAD_DIM // 4
ROPE_HALF = ROPE_DIMS // 2
LANES = 128
SUBLANES = 8
QK_SCALE = HEAD_DIM ** -0.5
SB_WIDTH = 512
DIFF_WIDTH = 512
DIFF_HEADS = 4
MOBA_WIDTH = 1024
MOBA_BLOCK = 256
MOBA_TOPK = 3
MOBA_TILE = 512
ATTN_BLOCK = 256
SB_TILE = 2048
DIFF_BLOCK = 512
DIFF_CHUNK = 512
SWEEP_UNROLL = 4
PROJ_CHUNK = 512
TOKEN_TILE = 512
ROPE_TILE = 2048
MASK_BIAS = -1e30
SB_CUTOFF = 110.0
VMEM_LIMIT = 48 * 1024 * 1024


def _params(n_axes):
    return pltpu.CompilerParams(dimension_semantics=("arbitrary",) * n_axes,
                                vmem_limit_bytes=VMEM_LIMIT)


def _resident(shape, index_map):
    return pl.BlockSpec(shape, index_map, pipeline_mode=pl.Buffered(1))


def _rope_table_kernel(pos_ref, invf_ref, cos_ref, sin_ref):
    ang = pos_ref[...].astype(F32) * invf_ref[...]
    lane = lax.broadcasted_iota(jnp.int32, ang.shape, 1) % HEAD_DIM
    c = jnp.cos(ang)
    s = jnp.sin(ang)
    cos_ref[...] = jnp.where(lane < ROPE_DIMS, c, 1.0)
    sin_ref[...] = jnp.where(lane < ROPE_HALF, -s, jnp.where(lane < ROPE_DIMS, s, 0.0))


def _rope_tables(positions):
    t = positions.size
    pos = positions.reshape(t, 1)
    inv_freq = ROPE_THETA ** (-(jnp.arange(0, ROPE_DIMS, 2, dtype=F32) / ROPE_DIMS))
    invf = jnp.tile(inv_freq, LANES // ROPE_HALF).reshape(1, LANES)
    tm = min(ROPE_TILE, t)
    return pl.pallas_call(
        _rope_table_kernel,
        grid=(t // tm,),
        in_specs=[pl.BlockSpec((tm, 1), lambda i: (i, 0)),
                  pl.BlockSpec((1, LANES), lambda i: (0, 0))],
        out_specs=[pl.BlockSpec((tm, LANES), lambda i: (i, 0))] * 2,
        out_shape=[jax.ShapeDtypeStruct((t, LANES), F32)] * 2,
        compiler_params=_params(1),
        name="rope_tables",
    )(pos, invf)


def _rope(a, cos, sin):
    lane = lax.broadcasted_iota(jnp.int32, a.shape, 1) % HEAD_DIM
    partner = jnp.where(lane < ROPE_HALF,
                        pltpu.roll(a, LANES - ROPE_HALF, 1),
                        pltpu.roll(a, ROPE_HALF, 1))
    return a * cos + partner * sin


def _rms(x, g):
    ms = jnp.mean(x * x, axis=-1, keepdims=True)
    return x * lax.rsqrt(ms + NORM_EPS) * g


def _norm_proj_kernel(x_ref, g_ref, w_ref, cos_ref, sin_ref, o_ref, *, rope_chunks, scale_chunks):
    u = _rms(x_ref[...], g_ref[...]).astype(BF16)
    n = w_ref.shape[1]
    for c in range(n // PROJ_CHUNK):
        cols = slice(c * PROJ_CHUNK, (c + 1) * PROJ_CHUNK)
        acc = jnp.dot(u, w_ref[:, cols], preferred_element_type=F32)
        if c in scale_chunks:
            acc = acc * QK_SCALE
        if c in rope_chunks:
            cos = cos_ref[...]
            sin = sin_ref[...]
            for gidx in range(PROJ_CHUNK // LANES):
                lanes = slice(gidx * LANES, (gidx + 1) * LANES)
                o_ref[:, c * PROJ_CHUNK + gidx * LANES:c * PROJ_CHUNK + (gidx + 1) * LANES] = (
                    _rope(acc[:, lanes], cos, sin).astype(BF16))
        else:
            o_ref[:, cols] = acc.astype(BF16)


def _norm_proj(h, g, w, cos_t, sin_t, rope_chunks, scale_chunks):
    t, d = h.shape
    n = w.shape[1]
    tm = min(TOKEN_TILE, t)
    kern = functools.partial(_norm_proj_kernel, rope_chunks=rope_chunks, scale_chunks=scale_chunks)
    return pl.pallas_call(
        kern,
        grid=(t // tm,),
        in_specs=[pl.BlockSpec((tm, d), lambda i: (i, 0)),
                  _resident((1, d), lambda i: (0, 0)),
                  _resident((d, n), lambda i: (0, 0)),
                  pl.BlockSpec((tm, LANES), lambda i: (i, 0)),
                  pl.BlockSpec((tm, LANES), lambda i: (i, 0))],
        out_specs=pl.BlockSpec((tm, n), lambda i: (i, 0)),
        out_shape=jax.ShapeDtypeStruct((t, n), BF16),
        compiler_params=_params(1),
        name="norm_proj",
    )(h, g.reshape(1, d), w, cos_t, sin_t)


def _head_lanes(shape, head):
    lane = lax.broadcasted_iota(jnp.int32, shape, 1)
    return (lane >= head * HEAD_DIM) & (lane < (head + 1) * HEAD_DIM)


def _stack_heads(q):
    return jnp.concatenate([jnp.where(_head_lanes(q.shape, 0), q, jnp.zeros_like(q)),
                            jnp.where(_head_lanes(q.shape, 1), q, jnp.zeros_like(q))], axis=0)


def _qk(q, k):
    return lax.dot_general(q, k, (((1,), (1,)), ((), ())), preferred_element_type=F32)


def _softmax_init(m_ref, acc_ref):
    m_ref[...] = jnp.full(m_ref.shape, -jnp.inf, F32)
    acc_ref[...] = jnp.zeros(acc_ref.shape, F32)


def _softmax_step(s, v, m_ref, acc_ref):
    m = m_ref[...]
    m_new = jnp.maximum(m, jnp.max(s, axis=-1, keepdims=True))
    alpha = jnp.exp(m - m_new)
    alpha = jnp.concatenate([alpha, alpha], axis=1)
    p = jnp.concatenate([jnp.exp(s[:, c * LANES:(c + 1) * LANES] - m_new)
                         for c in range(s.shape[1] // LANES)], axis=1).astype(BF16)
    v_ones = jnp.concatenate([v, jnp.ones_like(v)], axis=1)
    half = p.shape[0] // 2
    acc_ref[:half] = alpha[:half] * acc_ref[:half] + jnp.dot(p[:half], v_ones, preferred_element_type=F32)
    acc_ref[half:] = alpha[half:] * acc_ref[half:] + jnp.dot(p[half:], v_ones, preferred_element_type=F32)
    m_ref[...] = m_new


def _softmax_sweep(score_fn, v_fn, n_chunks, s_refs, m_ref, acc_ref):
    unroll = SWEEP_UNROLL

    def run(first, count, then_prefetch):
        for u in range(count):
            if u + 1 < count or then_prefetch:
                s_refs[(u + 1) % 2][...] = score_fn(first + u + 1)
            _softmax_step(s_refs[u % 2][...], v_fn(first + u), m_ref, acc_ref)

    def body(i, carry):
        run(unroll * i, unroll, True)
        return carry

    n_iters = jnp.maximum((n_chunks - 1) // unroll, 0)
    lax.fori_loop(0, n_iters, body, 0)
    last = unroll * n_iters

    for remaining in range(1, unroll + 1):
        @pl.when((n_chunks > 0) & (n_chunks - last == remaining))
        def _(remaining=remaining):
            run(last, remaining, False)


def _sweep_prefetch(score_fn, s_refs):
    s_refs[0][...] = score_fn(0)


def _softmax_result(acc_ref):
    return acc_ref[:, :LANES] / acc_ref[:, LANES:]


def _sb_kernel(q_ref, k_ref, v_ref, o_ref):
    blk = ATTN_BLOCK
    row = lax.broadcasted_iota(jnp.int32, (2 * blk, blk), 0) % blk
    col = lax.broadcasted_iota(jnp.int32, (2 * blk, blk), 1)
    past = col < row
    later = (lax.broadcasted_iota(jnp.int32, (blk, blk), 0)
             > lax.broadcasted_iota(jnp.int32, (blk, blk), 1)).astype(BF16)

    def scores(q2, j):
        z = _qk(q2, k_ref[0, pl.ds(j * blk, blk), :])
        return z, -(jnp.maximum(z, 0.0) + jnp.log(1.0 + jnp.exp(-jnp.abs(z))))

    def tail_in_block(log_keep):
        return jnp.dot(log_keep.astype(BF16), later, preferred_element_type=F32)

    def weighted_values(j, w):
        return jnp.dot(w.astype(BF16), v_ref[0, pl.ds(j * blk, blk), :], preferred_element_type=F32)

    def first_two_blocks(q2, qi):
        prev = jnp.maximum(qi - 1, 0)
        z_d, log_keep_d_all = scores(q2, qi)
        z_p, log_keep_p = scores(q2, prev)
        log_keep_d = jnp.where(past, log_keep_d_all, 0.0)
        c_d = jnp.sum(log_keep_d, axis=-1, keepdims=True)
        w_d = jnp.where(past, jnp.exp(log_keep_d_all + z_d + tail_in_block(log_keep_d)), 0.0)
        w_p = jnp.exp(log_keep_p + z_p + (tail_in_block(log_keep_p) + c_d))
        w_p = jnp.where(qi > 0, w_p, 0.0)
        acc = weighted_values(qi, w_d) + weighted_values(prev, w_p)
        return c_d + jnp.sum(log_keep_p, axis=-1, keepdims=True), acc

    def earlier_blocks(q2, qi, c, acc):
        def more(carry):
            j, c, _ = carry
            return (j >= 0) & (jnp.max(c) > -SB_CUTOFF)

        def body(carry):
            j, c, acc = carry
            z, log_keep = scores(q2, j)
            w = jnp.exp(log_keep + z + (tail_in_block(log_keep) + c))
            return j - 1, c + jnp.sum(log_keep, axis=-1, keepdims=True), acc + weighted_values(j, w)

        return lax.while_loop(more, body, (qi - 2, c, acc))[2]

    blocks = []
    for sub in range(SB_TILE // blk):
        qi = pl.program_id(2) * (SB_TILE // blk) + sub
        q2 = _stack_heads(q_ref[0, sub * blk:(sub + 1) * blk, :])
        blocks.append((q2, qi) + first_two_blocks(q2, qi))
    for sub, (q2, qi, c, acc) in enumerate(blocks):
        acc = earlier_blocks(q2, qi, c, acc)
        o_ref[0, sub * blk:(sub + 1) * blk, :] = jnp.where(
            _head_lanes((blk, LANES), 0), acc[:blk], acc[blk:]).astype(o_ref.dtype)


def _sb_attention(proj, n_batch, seq):
    groups = SB_WIDTH // LANES
    return pl.pallas_call(
        _sb_kernel,
        grid=(n_batch, groups, seq // SB_TILE),
        in_specs=[pl.BlockSpec((1, SB_TILE, LANES), lambda b, g, i: (b, i, g)),
                  pl.BlockSpec((1, seq, LANES), lambda b, g, i: (b, 0, groups + g)),
                  pl.BlockSpec((1, seq, LANES), lambda b, g, i: (b, 0, 2 * groups + g))],
        out_specs=pl.BlockSpec((1, SB_TILE, LANES), lambda b, g, i: (b, i, g)),
        out_shape=jax.ShapeDtypeStruct((n_batch, seq, SB_WIDTH), BF16),
        compiler_params=_params(3),
        name="sb_attention",
    )(proj, proj, proj)


def _diff_kernel(q_ref, k_ref, v_ref, lq1_ref, lk1_ref, lq2_ref, lk2_ref, g_ref, o_ref,
                 s0_ref, s1_ref, m_ref, acc_ref, *, lam_init):
    head = pl.program_id(1)
    qi = pl.program_id(2)
    blk = DIFF_BLOCK
    tk = DIFF_CHUNK
    q2 = _stack_heads(q_ref[0])
    row = lax.broadcasted_iota(jnp.int32, (2 * blk, tk), 0) % blk
    col = lax.broadcasted_iota(jnp.int32, (2 * blk, tk), 1)

    past_scores = lambda j: _qk(q2, k_ref[0, pl.ds(j * tk, tk), :])

    _softmax_init(m_ref, acc_ref)
    for c in range(blk // tk):
        keys = pl.ds(qi * blk + c * tk, tk)
        s = jnp.where(col + c * tk <= row, _qk(q2, k_ref[0, keys, :]), -jnp.inf)
        if c == blk // tk - 1:
            _sweep_prefetch(past_scores, (s0_ref, s1_ref))
        _softmax_step(s, v_ref[0, keys, :], m_ref, acc_ref)

    _softmax_sweep(past_scores, lambda j: v_ref[0, pl.ds(j * tk, tk), :],
                   qi * (blk // tk), (s0_ref, s1_ref), m_ref, acc_ref)
    normed = _softmax_result(acc_ref)

    def lam_term(a_ref, b_ref):
        prod = a_ref[pl.ds(head, 1), :] * b_ref[pl.ds(head, 1), :]
        return jnp.exp(jnp.sum(prod, axis=-1, keepdims=True))

    lam = lam_term(lq1_ref, lk1_ref) - lam_term(lq2_ref, lk2_ref) + lam_init
    o = normed[:blk] - lam * normed[blk:]
    o_ref[0] = (_rms(o, g_ref[...]) * (1.0 - lam_init)).astype(o_ref.dtype)


def _diff_attention(proj, n_batch, seq, lq1, lk1, lq2, lk2, subln, lam_init):
    q0 = 3 * SB_WIDTH // LANES
    k0 = q0 + DIFF_WIDTH // LANES
    v0 = k0 + DIFF_WIDTH // LANES
    lam_spec = _resident((DIFF_HEADS, HEAD_DIM), lambda b, h, i: (0, 0))
    return pl.pallas_call(
        functools.partial(_diff_kernel, lam_init=lam_init),
        grid=(n_batch, DIFF_HEADS, seq // DIFF_BLOCK),
        in_specs=[pl.BlockSpec((1, DIFF_BLOCK, LANES), lambda b, h, i: (b, i, q0 + h)),
                  pl.BlockSpec((1, seq, LANES), lambda b, h, i: (b, 0, k0 + h)),
                  pl.BlockSpec((1, seq, LANES), lambda b, h, i: (b, 0, v0 + h)),
                  lam_spec, lam_spec, lam_spec, lam_spec,
                  _resident((1, LANES), lambda b, h, i: (0, 0))],
        out_specs=pl.BlockSpec((1, DIFF_BLOCK, LANES), lambda b, h, i: (b, i, h)),
        out_shape=jax.ShapeDtypeStruct((n_batch, seq, DIFF_WIDTH), BF16),
        scratch_shapes=[pltpu.VMEM((2 * DIFF_BLOCK, DIFF_CHUNK), F32),
                        pltpu.VMEM((2 * DIFF_BLOCK, DIFF_CHUNK), F32),
                        pltpu.VMEM((2 * DIFF_BLOCK, LANES), F32),
                        pltpu.VMEM((2 * DIFF_BLOCK, 2 * LANES), F32)],
        compiler_params=_params(3),
        name="diff_attention",
    )(proj, proj, proj, lq1, lk1, lq2, lk2, subln.reshape(1, LANES))


def _moba_block_bias(q2, kmean_ref, tile_idx, n_blocks):
    n_rows = q2.shape[0]
    tile = n_rows // 2
    km = kmean_ref[...]
    km_hi = km.astype(BF16)
    km_lo = (km - km_hi.astype(F32)).astype(BF16)
    gate = _qk(km_hi, q2) + _qk(km_lo, q2)
    rows = -(-n_blocks // SUBLANES) * SUBLANES
    q_pos = lax.broadcasted_iota(jnp.int32, (rows, n_rows), 1) % tile + tile_idx * tile
    own_blk = q_pos // MOBA_BLOCK
    blk_id = lax.broadcasted_iota(jnp.int32, (rows, n_rows), 0)
    blk_f = blk_id.astype(F32)
    g = jnp.where(blk_id < own_blk, gate[:rows], -jnp.inf)
    chosen = blk_id == own_blk
    for _ in range(MOBA_TOPK):
        top = jnp.max(g, axis=0, keepdims=True)
        first = jnp.min(jnp.where(g == top, blk_f, float(2 * LANES)), axis=0, keepdims=True)
        pick = (blk_f == first) & (top > -jnp.inf)
        chosen = chosen | pick
        g = jnp.where(pick, -jnp.inf, g)
    bias = jnp.where(chosen, 0.0, MASK_BIAS)
    if rows < LANES:
        bias = jnp.concatenate([bias, jnp.zeros((LANES - rows, n_rows), F32)], axis=0)
    return bias.T.astype(BF16)


def _moba_kernel(q_ref, q_next_ref, k_ref, v_ref, o_ref, kmean_ref, kext_ref, bias_ref,
                 s0_ref, s1_ref, m_ref, acc_ref, *, n_blocks):
    qi = pl.program_id(2)
    tile = MOBA_TILE
    seq = k_ref.shape[1]

    @pl.when(qi == 0)
    def _():
        kmean_ref[...] = jnp.zeros_like(kmean_ref)

        def fill_mean(b, carry):
            kb = k_ref[0, pl.ds(b * MOBA_BLOCK, MOBA_BLOCK), :].astype(F32)
            kmean_ref[pl.ds(b, 1), :] = jnp.sum(kb, axis=0, keepdims=True) * (1.0 / MOBA_BLOCK)
            return carry

        lax.fori_loop(0, n_blocks, fill_mean, 0)

        def fill_ext(c, carry):
            rows = pl.ds(c * tile, tile)
            key_blk = (c * tile + lax.broadcasted_iota(jnp.int32, (tile, LANES), 0)) // MOBA_BLOCK
            lane = lax.broadcasted_iota(jnp.int32, (tile, LANES), 1)
            kext_ref[rows, :LANES] = k_ref[0, rows, :]
            kext_ref[rows, LANES:] = jnp.where(lane == key_blk, 1.0, 0.0).astype(BF16)
            return carry

        lax.fori_loop(0, seq // tile, fill_ext, 0)
        bias_ref[...] = _moba_block_bias(_stack_heads(q_ref[0]), kmean_ref, 0, n_blocks)

    q_ext = jnp.concatenate([_stack_heads(q_ref[0]), bias_ref[...]], axis=1)
    past_scores = lambda j: _qk(q_ext, kext_ref[pl.ds(j * tile, tile), :])
    bias_ref[...] = _moba_block_bias(_stack_heads(q_next_ref[0]), kmean_ref, qi + 1, n_blocks)

    row = lax.broadcasted_iota(jnp.int32, (2 * tile, tile), 0) % tile
    col = lax.broadcasted_iota(jnp.int32, (2 * tile, tile), 1)
    s = jnp.where(col <= row, _qk(q_ext, kext_ref[pl.ds(qi * tile, tile), :]), -jnp.inf)
    _sweep_prefetch(past_scores, (s0_ref, s1_ref))
    _softmax_init(m_ref, acc_ref)
    _softmax_step(s, v_ref[0, pl.ds(qi * tile, tile), :], m_ref, acc_ref)

    _softmax_sweep(past_scores, lambda j: v_ref[0, pl.ds(j * tile, tile), :],
                   qi, (s0_ref, s1_ref), m_ref, acc_ref)
    o = _softmax_result(acc_ref)
    o_ref[0] = jnp.where(_head_lanes((tile, LANES), 0), o[:tile], o[tile:]).astype(o_ref.dtype)


def _moba_attention(proj, n_batch, seq):
    groups = MOBA_WIDTH // LANES
    n_blocks = seq // MOBA_BLOCK
    n_tiles = seq // MOBA_TILE
    assert seq % MOBA_TILE == 0 and n_blocks <= LANES, "the block choice rides on 128 bias lanes"
    return pl.pallas_call(
        functools.partial(_moba_kernel, n_blocks=n_blocks),
        grid=(n_batch, groups, n_tiles),
        in_specs=[pl.BlockSpec((1, MOBA_TILE, LANES), lambda b, g, i: (b, i, g)),
                  pl.BlockSpec((1, MOBA_TILE, LANES),
                               lambda b, g, i: (b, jnp.minimum(i + 1, n_tiles - 1), g)),
                  pl.BlockSpec((1, seq, LANES), lambda b, g, i: (b, 0, groups + g)),
                  pl.BlockSpec((1, seq, LANES), lambda b, g, i: (b, 0, 2 * groups + g))],
        out_specs=pl.BlockSpec((1, MOBA_TILE, LANES), lambda b, g, i: (b, i, g)),
        out_shape=jax.ShapeDtypeStruct((n_batch, seq, MOBA_WIDTH), BF16),
        scratch_shapes=[pltpu.VMEM((LANES, LANES), F32),
                        pltpu.VMEM((seq, 2 * LANES), BF16),
                        pltpu.VMEM((2 * MOBA_TILE, LANES), BF16),
                        pltpu.VMEM((2 * MOBA_TILE, MOBA_TILE), F32),
                        pltpu.VMEM((2 * MOBA_TILE, MOBA_TILE), F32),
                        pltpu.VMEM((2 * MOBA_TILE, LANES), F32),
                        pltpu.VMEM((2 * MOBA_TILE, 2 * LANES), F32)],
        compiler_params=_params(3),
        name="moba_attention",
    )(proj, proj, proj, proj)


def _post_attention_kernel(h_ref, a_ref, b_ref, p_ref, wa_ref, wb_ref, g_mlp_ref, w1_ref, w2_ref,
                           g_ple_ref, wg_ref, wp_ref, g_final_ref, o_ref, *, final_norm):
    h = (h_ref[...]
         + jnp.dot(a_ref[...], wa_ref[...], preferred_element_type=F32)
         + jnp.dot(b_ref[...], wb_ref[...], preferred_element_type=F32))
    u = _rms(h, g_mlp_ref[...]).astype(BF16)
    o_ref[...] = h
    for c in range(w1_ref.shape[1] // PROJ_CHUNK):
        cols = slice(c * PROJ_CHUNK, (c + 1) * PROJ_CHUNK)
        a = jnp.maximum(jnp.dot(u, w1_ref[:, cols], preferred_element_type=F32), 0.0)
        o_ref[...] += jnp.dot((a * a).astype(BF16), w2_ref[cols, :], preferred_element_type=F32)
    h = o_ref[...]
    u = _rms(h, g_ple_ref[...]).astype(BF16)
    gate = jax.nn.sigmoid(jnp.dot(u, wg_ref[...], preferred_element_type=F32))
    emb = jnp.dot(p_ref[...].astype(BF16), wp_ref[...], preferred_element_type=F32)
    out = h + gate * emb
    if final_norm:
        out = _rms(out, g_final_ref[...])
    o_ref[...] = out


def _post_attention(h, a, b, a_col, b_col, p, w_out, g_mlp, w1, w2, g_ple, wg, wp, g_final, final_norm):
    t, d = h.shape
    half = w_out.shape[0] // 2
    dp = p.shape[1]
    d_ff = w1.shape[1]
    tm = min(TOKEN_TILE, t)
    row = lambda i: (i, 0)
    fixed = lambda i: (0, 0)
    return pl.pallas_call(
        functools.partial(_post_attention_kernel, final_norm=final_norm),
        grid=(t // tm,),
        in_specs=[pl.BlockSpec((tm, d), row),
                  pl.BlockSpec((tm, half), lambda i: (i, a_col)),
                  pl.BlockSpec((tm, half), lambda i: (i, b_col)),
                  pl.BlockSpec((tm, dp), row),
                  _resident((half, d), fixed),
                  _resident((half, d), lambda i: (1, 0)),
                  _resident((1, d), fixed),
                  _resident((d, d_ff), fixed),
                  _resident((d_ff, d), fixed),
                  _resident((1, d), fixed),
                  _resident((d, d), fixed),
                  _resident((dp, d), fixed),
                  _resident((1, d), fixed)],
        out_specs=pl.BlockSpec((tm, d), row),
        out_shape=jax.ShapeDtypeStruct((t, d), F32),
        compiler_params=_params(1),
        name="post_attention",
    )(h, a, b, p, w_out, w_out, g_mlp.reshape(1, d), w1, w2, g_ple.reshape(1, d), wg, wp,
      g_final.reshape(1, d))


def kernel(x, p, positions, attn_norm, ab_w_in, ab_w_out, diff_lam_q1, diff_lam_k1, diff_lam_q2,
           diff_lam_k2, diff_subln, moba_w_in, moba_w_out, mlp_norm, w_ff1, w_ff2, ple_norm,
           ple_gate, ple_proj, final_norm):
    n_batch, seq, d = x.shape
    depth = p.shape[0]
    t = n_batch * seq
    assert seq % DIFF_BLOCK == 0 and seq % SB_TILE == 0 and t % TOKEN_TILE == 0
    cos_t, sin_t = _rope_tables(positions)
    h = x.reshape(t, d)
    even_rope, even_scale = (3, 4), (0, 3)
    odd_rope, odd_scale = (0, 1, 2, 3), (0, 1)
    for i in range(depth):
        j = i // 2
        if i % 2 == 0:
            proj = _norm_proj(h, attn_norm[i], ab_w_in[j].astype(BF16), cos_t, sin_t,
                              even_rope, even_scale).reshape(n_batch, seq, -1)
            lam_init = 0.8 - 0.6 * math.exp(-0.3 * i)
            o_a = _sb_attention(proj, n_batch, seq).reshape(t, SB_WIDTH)
            o_b = _diff_attention(proj, n_batch, seq, diff_lam_q1[j], diff_lam_k1[j], diff_lam_q2[j],
                                  diff_lam_k2[j], diff_subln[j], lam_init).reshape(t, DIFF_WIDTH)
            mixed, w_out = (o_a, o_b, 0, 0), ab_w_out[j]
        else:
            proj = _norm_proj(h, attn_norm[i], moba_w_in[j].astype(BF16), cos_t, sin_t,
                              odd_rope, odd_scale).reshape(n_batch, seq, -1)
            o = _moba_attention(proj, n_batch, seq).reshape(t, MOBA_WIDTH)
            mixed, w_out = (o, o, 0, 1), moba_w_out[j]
        h = _post_attention(h, *mixed, p[i].reshape(t, -1), w_out.astype(BF16), mlp_norm[i],
                            w_ff1[i].astype(BF16), w_ff2[i].astype(BF16), ple_norm[i],
                            ple_gate[i].astype(BF16), ple_proj[i].astype(BF16), final_norm,
                            i == depth - 1)
    return h.reshape(n_batch, seq, d)
```

```python
import functools
import math

import jax
import jax.numpy as jnp
from jax import lax
from jax.experimental import pallas as pl
from jax.experimental.pallas import tpu as pltpu

F32 = jnp.float32
BF16 = jnp.bfloat16

NORM_EPS = 1e-6
ROPE_THETA = 500000.0
HEAD_DIM = 64
ROPE_DIMS = HEAD_DIM // 4
ROPE_HALF = ROPE_DIMS // 2
LANES = 128
SUBLANES = 8
QK_SCALE = HEAD_DIM ** -0.5
SB_WIDTH = 512
DIFF_WIDTH = 512
DIFF_HEADS = 4
MOBA_WIDTH = 1024
MOBA_BLOCK = 256
MOBA_TOPK = 3
MOBA_TILE = 512
ATTN_BLOCK = 256
SB_TILE = 2048
DIFF_BLOCK = 512
DIFF_CHUNK = 512
SWEEP_UNROLL = 4
PROJ_CHUNK = 512
TOKEN_TILE = 512
ROPE_TILE = 2048
MASK_BIAS = -1e30
SB_CUTOFF = 110.0
VMEM_LIMIT = 48 * 1024 * 1024


def _params(n_axes):
    return pltpu.CompilerParams(dimension_semantics=("arbitrary",) * n_axes,
                                vmem_limit_bytes=VMEM_LIMIT)


def _resident(shape, index_map):
    return pl.BlockSpec(shape, index_map, pipeline_mode=pl.Buffered(1))


def _rope_table_kernel(pos_ref, invf_ref, cos_ref, sin_ref):
    ang = pos_ref[...].astype(F32) * invf_ref[...]
    lane = lax.broadcasted_iota(jnp.int32, ang.shape, 1) % HEAD_DIM
    c = jnp.cos(ang)
    s = jnp.sin(ang)
    cos_ref[...] = jnp.where(lane < ROPE_DIMS, c, 1.0)
    sin_ref[...] = jnp.where(lane < ROPE_HALF, -s, jnp.where(lane < ROPE_DIMS, s, 0.0))


def _rope_tables(positions):
    t = positions.size
    pos = positions.reshape(t, 1)
    inv_freq = ROPE_THETA ** (-(jnp.arange(0, ROPE_DIMS, 2, dtype=F32) / ROPE_DIMS))
    invf = jnp.tile(inv_freq, LANES // ROPE_HALF).reshape(1, LANES)
    tm = min(ROPE_TILE, t)
    return pl.pallas_call(
        _rope_table_kernel,
        grid=(t // tm,),
        in_specs=[pl.BlockSpec((tm, 1), lambda i: (i, 0)),
                  pl.BlockSpec((1, LANES), lambda i: (0, 0))],
        out_specs=[pl.BlockSpec((tm, LANES), lambda i: (i, 0))] * 2,
        out_shape=[jax.ShapeDtypeStruct((t, LANES), F32)] * 2,
        compiler_params=_params(1),
        name="rope_tables",
    )(pos, invf)


def _rope(a, cos, sin):
    lane = lax.broadcasted_iota(jnp.int32, a.shape, 1) % HEAD_DIM
    partner = jnp.where(lane < ROPE_HALF,
                        pltpu.roll(a, LANES - ROPE_HALF, 1),
                        pltpu.roll(a, ROPE_HALF, 1))
    return a * cos + partner * sin


def _rms(x, g):
    ms = jnp.mean(x * x, axis=-1, keepdims=True)
    return x * lax.rsqrt(ms + NORM_EPS) * g


def _norm_proj_kernel(x_ref, g_ref, w_ref, cos_ref, sin_ref, o_ref, *, rope_chunks, scale_chunks):
    u = _rms(x_ref[...], g_ref[...]).astype(BF16)
    n = w_ref.shape[1]
    for c in range(n // PROJ_CHUNK):
        cols = slice(c * PROJ_CHUNK, (c + 1) * PROJ_CHUNK)
        acc = jnp.dot(u, w_ref[:, cols], preferred_element_type=F32)
        if c in scale_chunks:
            acc = acc * QK_SCALE
        if c in rope_chunks:
            cos = cos_ref[...]
            sin = sin_ref[...]
            for gidx in range(PROJ_CHUNK // LANES):
                lanes = slice(gidx * LANES, (gidx + 1) * LANES)
                o_ref[:, c * PROJ_CHUNK + gidx * LANES:c * PROJ_CHUNK + (gidx + 1) * LANES] = (
                    _rope(acc[:, lanes], cos, sin).astype(BF16))
        else:
            o_ref[:, cols] = acc.astype(BF16)


def _norm_proj(h, g, w, cos_t, sin_t, rope_chunks, scale_chunks):
    t, d = h.shape
    n = w.shape[1]
    tm = min(TOKEN_TILE, t)
    kern = functools.partial(_norm_proj_kernel, rope_chunks=rope_chunks, scale_chunks=scale_chunks)
    return pl.pallas_call(
        kern,
        grid=(t // tm,),
        in_specs=[pl.BlockSpec((tm, d), lambda i: (i, 0)),
                  _resident((1, d), lambda i: (0, 0)),
                  _resident((d, n), lambda i: (0, 0)),
                  pl.BlockSpec((tm, LANES), lambda i: (i, 0)),
                  pl.BlockSpec((tm, LANES), lambda i: (i, 0))],
        out_specs=pl.BlockSpec((tm, n), lambda i: (i, 0)),
        out_shape=jax.ShapeDtypeStruct((t, n), BF16),
        compiler_params=_params(1),
        name="norm_proj",
    )(h, g.reshape(1, d), w, cos_t, sin_t)


def _head_lanes(shape, head):
    lane = lax.broadcasted_iota(jnp.int32, shape, 1)
    return (lane >= head * HEAD_DIM) & (lane < (head + 1) * HEAD_DIM)


def _stack_heads(q):
    return jnp.concatenate([jnp.where(_head_lanes(q.shape, 0), q, jnp.zeros_like(q)),
                            jnp.where(_head_lanes(q.shape, 1), q, jnp.zeros_like(q))], axis=0)


def _qk(q, k):
    return lax.dot_general(q, k, (((1,), (1,)), ((), ())), preferred_element_type=F32)


def _softmax_init(m_ref, acc_ref):
    m_ref[...] = jnp.full(m_ref.shape, -jnp.inf, F32)
    acc_ref[...] = jnp.zeros(acc_ref.shape, F32)


def _softmax_step(s, v, m_ref, acc_ref):
    m = m_ref[...]
    m_new = jnp.maximum(m, jnp.max(s, axis=-1, keepdims=True))
    alpha = jnp.exp(m - m_new)
    alpha = jnp.concatenate([alpha, alpha], axis=1)
    p = jnp.concatenate([jnp.exp(s[:, c * LANES:(c + 1) * LANES] - m_new)
                         for c in range(s.shape[1] // LANES)], axis=1).astype(BF16)
    v_ones = jnp.concatenate([v, jnp.ones_like(v)], axis=1)
    half = p.shape[0] // 2
    acc_ref[:half] = alpha[:half] * acc_ref[:half] + jnp.dot(p[:half], v_ones, preferred_element_type=F32)
    acc_ref[half:] = alpha[half:] * acc_ref[half:] + jnp.dot(p[half:], v_ones, preferred_element_type=F32)
    m_ref[...] = m_new


def _softmax_sweep(score_fn, v_fn, n_chunks, s_refs, m_ref, acc_ref):
    unroll = SWEEP_UNROLL

    def run(first, count, then_prefetch):
        for u in range(count):
            if u + 1 < count or then_prefetch:
                s_refs[(u + 1) % 2][...] = score_fn(first + u + 1)
            _softmax_step(s_refs[u % 2][...], v_fn(first + u), m_ref, acc_ref)

    def body(i, carry):
        run(unroll * i, unroll, True)
        return carry

    n_iters = jnp.maximum((n_chunks - 1) // unroll, 0)
    lax.fori_loop(0, n_iters, body, 0)
    last = unroll * n_iters

    for remaining in range(1, unroll + 1):
        @pl.when((n_chunks > 0) & (n_chunks - last == remaining))
        def _(remaining=remaining):
            run(last, remaining, False)


def _sweep_prefetch(score_fn, s_refs):
    s_refs[0][...] = score_fn(0)


def _softmax_result(acc_ref):
    return acc_ref[:, :LANES] / acc_ref[:, LANES:]


def _sb_kernel(q_ref, k_ref, v_ref, o_ref):
    blk = ATTN_BLOCK
    row = lax.broadcasted_iota(jnp.int32, (2 * blk, blk), 0) % blk
    col = lax.broadcasted_iota(jnp.int32, (2 * blk, blk), 1)
    past = col < row
    later = (lax.broadcasted_iota(jnp.int32, (blk, blk), 0)
             > lax.broadcasted_iota(jnp.int32, (blk, blk), 1)).astype(BF16)

    def scores(q2, j):
        z = _qk(q2, k_ref[0, pl.ds(j * blk, blk), :])
        return z, -(jnp.maximum(z, 0.0) + jnp.log(1.0 + jnp.exp(-jnp.abs(z))))

    def tail_in_block(log_keep):
        return jnp.dot(log_keep.astype(BF16), later, preferred_element_type=F32)

    def weighted_values(j, w):
        return jnp.dot(w.astype(BF16), v_ref[0, pl.ds(j * blk, blk), :], preferred_element_type=F32)

    def first_two_blocks(q2, qi):
        prev = jnp.maximum(qi - 1, 0)
        z_d, log_keep_d_all = scores(q2, qi)
        z_p, log_keep_p = scores(q2, prev)
        log_keep_d = jnp.where(past, log_keep_d_all, 0.0)
        c_d = jnp.sum(log_keep_d, axis=-1, keepdims=True)
        w_d = jnp.where(past, jnp.exp(log_keep_d_all + z_d + tail_in_block(log_keep_d)), 0.0)
        w_p = jnp.exp(log_keep_p + z_p + (tail_in_block(log_keep_p) + c_d))
        w_p = jnp.where(qi > 0, w_p, 0.0)
        acc = weighted_values(qi, w_d) + weighted_values(prev, w_p)
        return c_d + jnp.sum(log_keep_p, axis=-1, keepdims=True), acc

    def earlier_blocks(q2, qi, c, acc):
        def more(carry):
            j, c, _ = carry
            return (j >= 0) & (jnp.max(c) > -SB_CUTOFF)

        def body(carry):
            j, c, acc = carry
            z, log_keep = scores(q2, j)
            w = jnp.exp(log_keep + z + (tail_in_block(log_keep) + c))
            return j - 1, c + jnp.sum(log_keep, axis=-1, keepdims=True), acc + weighted_values(j, w)

        return lax.while_loop(more, body, (qi - 2, c, acc))[2]

    blocks = []
    for sub in range(SB_TILE // blk):
        qi = pl.program_id(2) * (SB_TILE // blk) + sub
        q2 = _stack_heads(q_ref[0, sub * blk:(sub + 1) * blk, :])
        blocks.append((q2, qi) + first_two_blocks(q2, qi))
    for sub, (q2, qi, c, acc) in enumerate(blocks):
        acc = earlier_blocks(q2, qi, c, acc)
        o_ref[0, sub * blk:(sub + 1) * blk, :] = jnp.where(
            _head_lanes((blk, LANES), 0), acc[:blk], acc[blk:]).astype(o_ref.dtype)


def _sb_attention(proj, n_batch, seq):
    groups = SB_WIDTH // LANES
    return pl.pallas_call(
        _sb_kernel,
        grid=(n_batch, groups, seq // SB_TILE),
        in_specs=[pl.BlockSpec((1, SB_TILE, LANES), lambda b, g, i: (b, i, g)),
                  pl.BlockSpec((1, seq, LANES), lambda b, g, i: (b, 0, groups + g)),
                  pl.BlockSpec((1, seq, LANES), lambda b, g, i: (b, 0, 2 * groups + g))],
        out_specs=pl.BlockSpec((1, SB_TILE, LANES), lambda b, g, i: (b, i, g)),
        out_shape=jax.ShapeDtypeStruct((n_batch, seq, SB_WIDTH), BF16),
        compiler_params=_params(3),
        name="sb_attention",
    )(proj, proj, proj)


def _diff_kernel(q_ref, k_ref, v_ref, lq1_ref, lk1_ref, lq2_ref, lk2_ref, g_ref, o_ref,
                 s0_ref, s1_ref, m_ref, acc_ref, *, lam_init):
    head = pl.program_id(1)
    qi = pl.program_id(2)
    blk = DIFF_BLOCK
    tk = DIFF_CHUNK
    q2 = _stack_heads(q_ref[0])
    row = lax.broadcasted_iota(jnp.int32, (2 * blk, tk), 0) % blk
    col = lax.broadcasted_iota(jnp.int32, (2 * blk, tk), 1)

    past_scores = lambda j: _qk(q2, k_ref[0, pl.ds(j * tk, tk), :])

    _softmax_init(m_ref, acc_ref)
    for c in range(blk // tk):
        keys = pl.ds(qi * blk + c * tk, tk)
        s = jnp.where(col + c * tk <= row, _qk(q2, k_ref[0, keys, :]), -jnp.inf)
        if c == blk // tk - 1:
            _sweep_prefetch(past_scores, (s0_ref, s1_ref))
        _softmax_step(s, v_ref[0, keys, :], m_ref, acc_ref)

    _softmax_sweep(past_scores, lambda j: v_ref[0, pl.ds(j * tk, tk), :],
                   qi * (blk // tk), (s0_ref, s1_ref), m_ref, acc_ref)
    normed = _softmax_result(acc_ref)

    def lam_term(a_ref, b_ref):
        prod = a_ref[pl.ds(head, 1), :] * b_ref[pl.ds(head, 1), :]
        return jnp.exp(jnp.sum(prod, axis=-1, keepdims=True))

    lam = lam_term(lq1_ref, lk1_ref) - lam_term(lq2_ref, lk2_ref) + lam_init
    o = normed[:blk] - lam * normed[blk:]
    o_ref[0] = (_rms(o, g_ref[...]) * (1.0 - lam_init)).astype(o_ref.dtype)


def _diff_attention(proj, n_batch, seq, lq1, lk1, lq2, lk2, subln, lam_init):
    q0 = 3 * SB_WIDTH // LANES
    k0 = q0 + DIFF_WIDTH // LANES
    v0 = k0 + DIFF_WIDTH // LANES
    lam_spec = _resident((DIFF_HEADS, HEAD_DIM), lambda b, h, i: (0, 0))
    return pl.pallas_call(
        functools.partial(_diff_kernel, lam_init=lam_init),
        grid=(n_batch, DIFF_HEADS, seq // DIFF_BLOCK),
        in_specs=[pl.BlockSpec((1, DIFF_BLOCK, LANES), lambda b, h, i: (b, i, q0 + h)),
                  pl.BlockSpec((1, seq, LANES), lambda b, h, i: (b, 0, k0 + h)),
                  pl.BlockSpec((1, seq, LANES), lambda b, h, i: (b, 0, v0 + h)),
                  lam_spec, lam_spec, lam_spec, lam_spec,
                  _resident((1, LANES), lambda b, h, i: (0, 0))],
        out_specs=pl.BlockSpec((1, DIFF_BLOCK, LANES), lambda b, h, i: (b, i, h)),
        out_shape=jax.ShapeDtypeStruct((n_batch, seq, DIFF_WIDTH), BF16),
        scratch_shapes=[pltpu.VMEM((2 * DIFF_BLOCK, DIFF_CHUNK), F32),
                        pltpu.VMEM((2 * DIFF_BLOCK, DIFF_CHUNK), F32),
                        pltpu.VMEM((2 * DIFF_BLOCK, LANES), F32),
                        pltpu.VMEM((2 * DIFF_BLOCK, 2 * LANES), F32)],
        compiler_params=_params(3),
        name="diff_attention",
    )(proj, proj, proj, lq1, lk1, lq2, lk2, subln.reshape(1, LANES))


def _moba_block_bias(q2, kmean_ref, tile_idx, n_blocks):
    n_rows = q2.shape[0]
    tile = n_rows // 2
    km = kmean_ref[...]
    km_hi = km.astype(BF16)
    km_lo = (km - km_hi.astype(F32)).astype(BF16)
    gate = _qk(km_hi, q2) + _qk(km_lo, q2)
    rows = -(-n_blocks // SUBLANES) * SUBLANES
    q_pos = lax.broadcasted_iota(jnp.int32, (rows, n_rows), 1) % tile + tile_idx * tile
    own_blk = q_pos // MOBA_BLOCK
    blk_id = lax.broadcasted_iota(jnp.int32, (rows, n_rows), 0)
    blk_f = blk_id.astype(F32)
    g = jnp.where(blk_id < own_blk, gate[:rows], -jnp.inf)
    chosen = blk_id == own_blk
    for _ in range(MOBA_TOPK):
        top = jnp.max(g, axis=0, keepdims=True)
        first = jnp.min(jnp.where(g == top, blk_f, float(2 * LANES)), axis=0, keepdims=True)
        pick = (blk_f == first) & (top > -jnp.inf)
        chosen = chosen | pick
        g = jnp.where(pick, -jnp.inf, g)
    bias = jnp.where(chosen, 0.0, MASK_BIAS)
    if rows < LANES:
        bias = jnp.concatenate([bias, jnp.zeros((LANES - rows, n_rows), F32)], axis=0)
    return bias.T.astype(BF16)


def _moba_kernel(q_ref, q_next_ref, k_ref, v_ref, o_ref, kmean_ref, one_hot_ref, bias_ref,
                 s0_ref, s1_ref, m_ref, acc_ref, *, n_blocks):
    qi = pl.program_id(2)
    tile = MOBA_TILE
    seq = k_ref.shape[1]

    @pl.when((pl.program_id(0) == 0) & (pl.program_id(1) == 0) & (qi == 0))
    def _():
        def fill_one_hot(c, carry):
            rows = pl.ds(c * tile, tile)
            key_blk = (c * tile + lax.broadcasted_iota(jnp.int32, (tile, LANES), 0)) // MOBA_BLOCK
            lane = lax.broadcasted_iota(jnp.int32, (tile, LANES), 1)
            one_hot_ref[rows, :] = jnp.where(lane == key_blk, 1.0, 0.0).astype(BF16)
            return carry

        lax.fori_loop(0, seq // tile, fill_one_hot, 0)

    @pl.when(qi == 0)
    def _():
        kmean_ref[...] = jnp.zeros_like(kmean_ref)

        def fill_mean(b, carry):
            kb = k_ref[0, pl.ds(b * MOBA_BLOCK, MOBA_BLOCK), :].astype(F32)
            kmean_ref[pl.ds(b, 1), :] = jnp.sum(kb, axis=0, keepdims=True) * (1.0 / MOBA_BLOCK)
            return carry

        lax.fori_loop(0, n_blocks, fill_mean, 0)

        bias_ref[...] = _moba_block_bias(_stack_heads(q_ref[0]), kmean_ref, 0, n_blocks)

    q_ext = jnp.concatenate([_stack_heads(q_ref[0]), bias_ref[...]], axis=1)
    k_ext = lambda j: jnp.concatenate([k_ref[0, pl.ds(j * tile, tile), :],
                                       one_hot_ref[pl.ds(j * tile, tile), :]], axis=1)
    past_scores = lambda j: _qk(q_ext, k_ext(j))
    bias_ref[...] = _moba_block_bias(_stack_heads(q_next_ref[0]), kmean_ref, qi + 1, n_blocks)

    row = lax.broadcasted_iota(jnp.int32, (2 * tile, tile), 0) % tile
    col = lax.broadcasted_iota(jnp.int32, (2 * tile, tile), 1)
    s = jnp.where(col <= row, _qk(q_ext, k_ext(qi)), -jnp.inf)
    _sweep_prefetch(past_scores, (s0_ref, s1_ref))
    _softmax_init(m_ref, acc_ref)
    _softmax_step(s, v_ref[0, pl.ds(qi * tile, tile), :], m_ref, acc_ref)

    _softmax_sweep(past_scores, lambda j: v_ref[0, pl.ds(j * tile, tile), :],
                   qi, (s0_ref, s1_ref), m_ref, acc_ref)
    o = _softmax_result(acc_ref)
    o_ref[0] = jnp.where(_head_lanes((tile, LANES), 0), o[:tile], o[tile:]).astype(o_ref.dtype)


def _moba_attention(proj, n_batch, seq):
    groups = MOBA_WIDTH // LANES
    n_blocks = seq // MOBA_BLOCK
    n_tiles = seq // MOBA_TILE
    assert seq % MOBA_TILE == 0 and n_blocks <= LANES, "the block choice rides on 128 bias lanes"
    return pl.pallas_call(
        functools.partial(_moba_kernel, n_blocks=n_blocks),
        grid=(n_batch, groups, n_tiles),
        in_specs=[pl.BlockSpec((1, MOBA_TILE, LANES), lambda b, g, i: (b, i, g)),
                  pl.BlockSpec((1, MOBA_TILE, LANES),
                               lambda b, g, i: (b, jnp.minimum(i + 1, n_tiles - 1), g)),
                  pl.BlockSpec((1, seq, LANES), lambda b, g, i: (b, 0, groups + g)),
                  pl.BlockSpec((1, seq, LANES), lambda b, g, i: (b, 0, 2 * groups + g))],
        out_specs=pl.BlockSpec((1, MOBA_TILE, LANES), lambda b, g, i: (b, i, g)),
        out_shape=jax.ShapeDtypeStruct((n_batch, seq, MOBA_WIDTH), BF16),
        scratch_shapes=[pltpu.VMEM((LANES, LANES), F32),
                        pltpu.VMEM((seq, LANES), BF16),
                        pltpu.VMEM((2 * MOBA_TILE, LANES), BF16),
                        pltpu.VMEM((2 * MOBA_TILE, MOBA_TILE), F32),
                        pltpu.VMEM((2 * MOBA_TILE, MOBA_TILE), F32),
                        pltpu.VMEM((2 * MOBA_TILE, LANES), F32),
                        pltpu.VMEM((2 * MOBA_TILE, 2 * LANES), F32)],
        compiler_params=_params(3),
        name="moba_attention",
    )(proj, proj, proj, proj)


def _post_attention_kernel(h_ref, a_ref, b_ref, p_ref, wa_ref, wb_ref, g_mlp_ref, w1_ref, w2_ref,
                           g_ple_ref, wg_ref, wp_ref, g_final_ref, o_ref, *, final_norm):
    h = (h_ref[...]
         + jnp.dot(a_ref[...], wa_ref[...], preferred_element_type=F32)
         + jnp.dot(b_ref[...], wb_ref[...], preferred_element_type=F32))
    u = _rms(h, g_mlp_ref[...]).astype(BF16)
    o_ref[...] = h
    for c in range(w1_ref.shape[1] // PROJ_CHUNK):
        cols = slice(c * PROJ_CHUNK, (c + 1) * PROJ_CHUNK)
        a = jnp.maximum(jnp.dot(u, w1_ref[:, cols], preferred_element_type=F32), 0.0)
        o_ref[...] += jnp.dot((a * a).astype(BF16), w2_ref[cols, :], preferred_element_type=F32)
    h = o_ref[...]
    u = _rms(h, g_ple_ref[...]).astype(BF16)
    gate = jax.nn.sigmoid(jnp.dot(u, wg_ref[...], preferred_element_type=F32))
    emb = jnp.dot(p_ref[...].astype(BF16), wp_ref[...], preferred_element_type=F32)
    out = h + gate * emb
    if final_norm:
        out = _rms(out, g_final_ref[...])
    o_ref[...] = out


def _post_attention(h, a, b, a_col, b_col, p, w_out, g_mlp, w1, w2, g_ple, wg, wp, g_final, final_norm):
    t, d = h.shape
    half = w_out.shape[0] // 2
    dp = p.shape[1]
    d_ff = w1.shape[1]
    tm = min(TOKEN_TILE, t)
    row = lambda i: (i, 0)
    fixed = lambda i: (0, 0)
    return pl.pallas_call(
        functools.partial(_post_attention_kernel, final_norm=final_norm),
        grid=(t // tm,),
        in_specs=[pl.BlockSpec((tm, d), row),
                  pl.BlockSpec((tm, half), lambda i: (i, a_col)),
                  pl.BlockSpec((tm, half), lambda i: (i, b_col)),
                  pl.BlockSpec((tm, dp), row),
                  _resident((half, d), fixed),
                  _resident((half, d), lambda i: (1, 0)),
                  _resident((1, d), fixed),
                  _resident((d, d_ff), fixed),
                  _resident((d_ff, d), fixed),
                  _resident((1, d), fixed),
                  _resident((d, d), fixed),
                  _resident((dp, d), fixed),
                  _resident((1, d), fixed)],
        out_specs=pl.BlockSpec((tm, d), row),
        out_shape=jax.ShapeDtypeStruct((t, d), F32),
        compiler_params=_params(1),
        name="post_attention",
    )(h, a, b, p, w_out, w_out, g_mlp.reshape(1, d), w1, w2, g_ple.reshape(1, d), wg, wp,
      g_final.reshape(1, d))


def kernel(x, p, positions, attn_norm, ab_w_in, ab_w_out, diff_lam_q1, diff_lam_k1, diff_lam_q2,
           diff_lam_k2, diff_subln, moba_w_in, moba_w_out, mlp_norm, w_ff1, w_ff2, ple_norm,
           ple_gate, ple_proj, final_norm):
    n_batch, seq, d = x.shape
    depth = p.shape[0]
    t = n_batch * seq
    assert seq % DIFF_BLOCK == 0 and seq % SB_TILE == 0 and t % TOKEN_TILE == 0
    cos_t, sin_t = _rope_tables(positions)
    h = x.reshape(t, d)
    even_rope, even_scale = (3, 4), (0, 3)
    odd_rope, odd_scale = (0, 1, 2, 3), (0, 1)
    for i in range(depth):
        j = i // 2
        if i % 2 == 0:
            proj = _norm_proj(h, attn_norm[i], ab_w_in[j].astype(BF16), cos_t, sin_t,
                              even_rope, even_scale).reshape(n_batch, seq, -1)
            lam_init = 0.8 - 0.6 * math.exp(-0.3 * i)
            o_a = _sb_attention(proj, n_batch, seq).reshape(t, SB_WIDTH)
            o_b = _diff_attention(proj, n_batch, seq, diff_lam_q1[j], diff_lam_k1[j], diff_lam_q2[j],
                                  diff_lam_k2[j], diff_subln[j], lam_init).reshape(t, DIFF_WIDTH)
            mixed, w_out = (o_a, o_b, 0, 0), ab_w_out[j]
        else:
            proj = _norm_proj(h, attn_norm[i], moba_w_in[j].astype(BF16), cos_t, sin_t,
                              odd_rope, odd_scale).reshape(n_batch, seq, -1)
            o = _moba_attention(proj, n_batch, seq).reshape(t, MOBA_WIDTH)
            mixed, w_out = (o, o, 0, 1), moba_w_out[j]
        h = _post_attention(h, *mixed, p[i].reshape(t, -1), w_out.astype(BF16), mlp_norm[i],
                            w_ff1[i].astype(BF16), w_ff2[i].astype(BF16), ple_norm[i],
                            ple_gate[i].astype(BF16), ple_proj[i].astype(BF16), final_norm,
                            i == depth - 1)
    return h.reshape(n_batch, seq, d)
```
